```python
import jax, jax.numpy as jnp
from jax import lax
import numpy as np

D_MODEL = 1024
BATCH = 4
SEQ = 4096
DEPTH = 4
DEC_BATCH = 128
DEC_SEQ = 4
PAST_LEN = 2048
PAGE_SIZE = 128

N_MIXERS = 2
N_SB_LAYERS = (DEPTH + 1) // 2
N_GLA_LAYERS = DEPTH // 2
SB_HEADS = 16
SB_HEAD_DIM = D_MODEL // SB_HEADS
SB_QBLOCK = 128
SB_BIAS_INIT = -8.0
GLA_HEADS = 4
GLA_DK = D_MODEL // 2
GLA_DV = D_MODEL
GLA_HK = GLA_DK // GLA_HEADS
GLA_HV = GLA_DV // GLA_HEADS
GLA_GATE_RANK = 16
GLA_GATE_TAU = 16.0
GLA_CHUNK = 64
GLA_SPLITS = (GLA_DK, 2 * GLA_DK, 2 * GLA_DK + GLA_DV, 2 * GLA_DK + 2 * GLA_DV)
GLA_IN = 2 * GLA_DK + 2 * GLA_DV + GLA_GATE_RANK
D_FF = ((8 * D_MODEL + 3 * 256 - 1) // (3 * 256)) * 256
PLE_DIM = 256
NORM_EPS = 1e-6

kernel_name = 'stickbreak_gla_hybrid_step'


def _rmsnorm(x, g):
    xf = x.astype(jnp.float32)
    y = xf * lax.rsqrt(jnp.mean(xf * xf, axis=-1, keepdims=True) + NORM_EPS) * g.astype(jnp.float32)
    return y.astype(x.dtype)


def _sb_block(q, k, v, bias, q_pos, k_pos):
    z = jnp.einsum('bqhd,bkhd->bhqk', q.astype(jnp.float32), k.astype(jnp.float32)) * (SB_HEAD_DIM ** -0.5)
    z = z + bias.astype(jnp.float32)[None, :, None, None]
    mask = (k_pos[None, :] < q_pos[:, None])[None, None]
    log_1mb = jnp.where(mask, jax.nn.log_sigmoid(-z), 0.0)
    tail = lax.cumsum(log_1mb, axis=3, reverse=True) - log_1mb
    w = jnp.where(mask, jnp.exp(jax.nn.log_sigmoid(z) + tail), 0.0)
    return jnp.einsum('bhqk,bkhd->bqhd', w, v.astype(jnp.float32))


def _stick_breaking(q, k, v, bias, q_pos, k_pos):
    B, T = q.shape[0], q.shape[1]
    if T > SB_QBLOCK and T % SB_QBLOCK == 0:
        nb = T // SB_QBLOCK
        qb = jnp.moveaxis(q.reshape(B, nb, SB_QBLOCK, SB_HEADS, SB_HEAD_DIM), 1, 0)
        pb = q_pos.reshape(nb, SB_QBLOCK)
        ob = lax.map(lambda a: _sb_block(a[0], k, v, bias, a[1], k_pos), (qb, pb))
        return jnp.moveaxis(ob, 0, 1).reshape(B, T, SB_HEADS, SB_HEAD_DIM)
    return _sb_block(q, k, v, bias, q_pos, k_pos)


def _gla_chunk(S, inp):
    q, k, v, la = inp
    C = q.shape[2]
    b = jnp.cumsum(la, axis=2)
    qg = q * jnp.exp(b)
    kg = k * jnp.exp(-b)
    causal = jnp.tril(jnp.ones((C, C), dtype=bool))
    a = jnp.where(causal, jnp.einsum('bhtd,bhsd->bhts', qg, kg), 0.0)
    o = jnp.einsum('bhts,bhsv->bhtv', a, v) + jnp.einsum('bhtd,bhdv->bhtv', qg, S)
    b_last = b[:, :, -1:, :]
    S_new = jnp.exp(b_last[:, :, 0, :])[..., None] * S + jnp.einsum('bhsd,bhsv->bhdv', k * jnp.exp(b_last - b), v)
    return S_new, o


def _gla(q, k, v, la, S0):
    B, H, T = q.shape[0], q.shape[1], q.shape[2]
    C = GLA_CHUNK if T % GLA_CHUNK == 0 else T
    n = T // C
    split = lambda t: jnp.moveaxis(t.reshape(B, H, n, C, t.shape[-1]), 2, 0)
    S, o = lax.scan(_gla_chunk, S0, (split(q), split(k), split(v), split(la)))
    o = jnp.moveaxis(o, 0, 2).reshape(B, H, T, GLA_HV)
    return o, S


def _trunk(x, p, cache_k, cache_v, page_table, state_gla, W):
    B, T = x.shape[0], x.shape[1]
    past = 0 if cache_k is None else page_table.shape[1] * PAGE_SIZE
    q_pos = past + jnp.arange(T, dtype=jnp.int32)
    k_pos = jnp.arange(past + T, dtype=jnp.int32)
    h = x
    new_k, new_v, new_s = [], [], []
    for i in range(DEPTH):
        j = i // N_MIXERS
        xn = _rmsnorm(h, W['g_mix'][i])
        if i % N_MIXERS == 0:
            qkv = (xn @ W['w_sb_in'][j]).reshape(B, T, 3, SB_HEADS, SB_HEAD_DIM)
            q, k, v = qkv[:, :, 0], qkv[:, :, 1], qkv[:, :, 2]
            new_k.append(k)
            new_v.append(v)
            if cache_k is not None:
                pk = jnp.take(cache_k[j], page_table, axis=0).reshape(B, past, SB_HEADS, SB_HEAD_DIM)
                pv = jnp.take(cache_v[j], page_table, axis=0).reshape(B, past, SB_HEADS, SB_HEAD_DIM)
                k = jnp.concatenate([pk.astype(k.dtype), k], axis=1)
                v = jnp.concatenate([pv.astype(v.dtype), v], axis=1)
            o = _stick_breaking(q, k, v, W['b_sb'][j], q_pos, k_pos).astype(h.dtype)
            h = h + o.reshape(B, T, D_MODEL) @ W['w_sb_out'][j]
        else:
            proj = xn @ W['w_gla_in'][j]
            q, k, v, g, a = jnp.split(proj, list(GLA_SPLITS), axis=-1)
            la = jax.nn.log_sigmoid((a @ W['w_gla_gate'][j] + W['b_gla_gate'][j]).astype(jnp.float32)) / GLA_GATE_TAU
            heads = lambda t, d: jnp.moveaxis(t.reshape(B, T, GLA_HEADS, d), 2, 1).astype(jnp.float32)
            if state_gla is None:
                S0 = jnp.zeros((B, GLA_HEADS, GLA_HK, GLA_HV), jnp.float32)
            else:
                S0 = state_gla[j].astype(jnp.float32)
            o, S = _gla(heads(q, GLA_HK) * (GLA_HK ** -0.5), heads(k, GLA_HK), heads(v, GLA_HV), heads(la, GLA_HK), S0)
            new_s.append(S.astype(x.dtype))
            o = jnp.moveaxis(o, 1, 2)
            o = _rmsnorm(o, W['g_gla_norm'][j]) * jax.nn.silu(g.astype(jnp.float32)).reshape(B, T, GLA_HEADS, GLA_HV)
            h = h + o.reshape(B, T, GLA_DV).astype(h.dtype) @ W['w_gla_out'][j]
        xn = _rmsnorm(h, W['g_ffn'][i])
        gu = xn @ W['w_ffn_in'][i]
        h = h + (jax.nn.silu(gu[..., :D_FF]) * gu[..., D_FF:]) @ W['w_ffn_out'][i]
        hn = _rmsnorm(h, W['g_ple'][i])
        h = h + jax.nn.sigmoid(hn @ W['w_ple_gate'][i]) * (p[i] @ W['w_ple_proj'][i])
    return _rmsnorm(h, W['g_final']), jnp.stack(new_k), jnp.stack(new_v), jnp.stack(new_s)


def setup_inputs(seed: int = 0) -> dict:
    key = jax.random.key(seed)
    ks = jax.random.split(key, 24)
    n_pages = PAST_LEN // PAGE_SIZE
    n_used = DEC_BATCH * n_pages
    n_pool = n_used + max(1, n_used // 4)
    nrm = lambda k, shape, s: jax.random.normal(k, shape, jnp.float32) * s
    gain = lambda k, shape: 1.0 + 0.02 * jax.random.normal(k, shape, jnp.float32)
    return {
        'x_prompt': nrm(ks[0], (BATCH, SEQ, D_MODEL), 1.0),
        'x_sample': nrm(ks[1], (DEC_BATCH, DEC_SEQ, D_MODEL), 1.0),
        'cache_sb_k': nrm(ks[2], (N_SB_LAYERS, n_pool, PAGE_SIZE, SB_HEADS, SB_HEAD_DIM), 1.0),
        'cache_sb_v': nrm(ks[3], (N_SB_LAYERS, n_pool, PAGE_SIZE, SB_HEADS, SB_HEAD_DIM), 1.0),
        'state_gla': nrm(ks[4], (N_GLA_LAYERS, DEC_BATCH, GLA_HEADS, GLA_HK, GLA_HV), 0.5),
        'page_table': jax.random.permutation(ks[5], n_pool)[:n_used].reshape(DEC_BATCH, n_pages).astype(jnp.int32),
        'p_prompt': nrm(ks[6], (DEPTH, BATCH, SEQ, PLE_DIM), 1.0),
        'p_sample': nrm(ks[7], (DEPTH, DEC_BATCH, DEC_SEQ, PLE_DIM), 1.0),
        'g_mix': gain(ks[8], (DEPTH, D_MODEL)),
        'w_sb_in': nrm(ks[9], (N_SB_LAYERS, D_MODEL, 3 * D_MODEL), D_MODEL ** -0.5),
        'w_sb_out': nrm(ks[10], (N_SB_LAYERS, D_MODEL, D_MODEL), D_MODEL ** -0.5),
        'b_sb': SB_BIAS_INIT + nrm(ks[23], (N_SB_LAYERS, SB_HEADS), 0.5),
        'w_gla_in': nrm(ks[11], (N_GLA_LAYERS, D_MODEL, GLA_IN), D_MODEL ** -0.5),
        'w_gla_gate': nrm(ks[12], (N_GLA_LAYERS, GLA_GATE_RANK, GLA_DK), GLA_GATE_RANK ** -0.5),
        'b_gla_gate': nrm(ks[13], (N_GLA_LAYERS, GLA_DK), 0.02),
        'g_gla_norm': gain(ks[14], (N_GLA_LAYERS, GLA_HV)),
        'w_gla_out': nrm(ks[15], (N_GLA_LAYERS, GLA_DV, D_MODEL), GLA_DV ** -0.5),
        'g_ffn': gain(ks[16], (DEPTH, D_MODEL)),
        'w_ffn_in': nrm(ks[17], (DEPTH, D_MODEL, 2 * D_FF), D_MODEL ** -0.5),
        'w_ffn_out': nrm(ks[18], (DEPTH, D_FF, D_MODEL), D_FF ** -0.5),
        'g_ple': gain(ks[19], (DEPTH, D_MODEL)),
        'w_ple_gate': nrm(ks[20], (DEPTH, D_MODEL, D_MODEL), D_MODEL ** -0.5),
        'w_ple_proj': nrm(ks[21], (DEPTH, PLE_DIM, D_MODEL), PLE_DIM ** -0.5),
        'g_final': gain(ks[22], (D_MODEL,)),
    }


def reference(x_prompt, x_sample, cache_sb_k, cache_sb_v, state_gla, page_table, p_prompt, p_sample,
              g_mix, w_sb_in, w_sb_out, b_sb, w_gla_in, w_gla_gate, b_gla_gate, g_gla_norm, w_gla_out,
              g_ffn, w_ffn_in, w_ffn_out, g_ple, w_ple_gate, w_ple_proj, g_final):
    W = dict(g_mix=g_mix, w_sb_in=w_sb_in, w_sb_out=w_sb_out, b_sb=b_sb, w_gla_in=w_gla_in,
             w_gla_gate=w_gla_gate, b_gla_gate=b_gla_gate, g_gla_norm=g_gla_norm, w_gla_out=w_gla_out,
             g_ffn=g_ffn, w_ffn_in=w_ffn_in, w_ffn_out=w_ffn_out, g_ple=g_ple, w_ple_gate=w_ple_gate,
             w_ple_proj=w_ple_proj, g_final=g_final)
    y_prompt, k_prompt, v_prompt, s_prompt = _trunk(x_prompt, p_prompt, None, None, None, None, W)
    y_sample, k_sample, v_sample, s_sample = _trunk(x_sample, p_sample, cache_sb_k, cache_sb_v, page_table, state_gla, W)
    return (y_prompt, y_sample, k_prompt, v_prompt, s_prompt, k_sample, v_sample, s_sample)
```

```python
import functools

import jax
import jax.numpy as jnp
from jax import lax
from jax.experimental import pallas as pl
from jax.experimental.pallas import tpu as pltpu

BF = jnp.bfloat16
F32 = jnp.float32

D_MODEL = 1024
DEPTH = 4
PAGE_SIZE = 128
SB_HEADS = 16
SB_HEAD_DIM = D_MODEL // SB_HEADS
GLA_HEADS = 4
GLA_DK = D_MODEL // 2
GLA_DV = D_MODEL
GLA_HK = GLA_DK // GLA_HEADS
GLA_HV = GLA_DV // GLA_HEADS
GLA_GATE_RANK = 16
GLA_GATE_TAU = 16.0
GLA_CHUNK = 64
D_FF = 2816
NORM_EPS = 1e-6

LANES = 128
SUBLANES = 8
VMEM_LIMIT = 56 * 1024 * 1024


def _params(*sem):
    return pltpu.CompilerParams(dimension_semantics=sem, vmem_limit_bytes=VMEM_LIMIT)


def _rms(x, g):
    return x * lax.rsqrt(jnp.mean(x * x, axis=-1, keepdims=True) + NORM_EPS) * g


def _dot(a, b):
    return jnp.dot(a, b, preferred_element_type=F32)


def _dot_nt(a, b):
    return lax.dot_general(a, b, (((1,), (1,)), ((), ())), preferred_element_type=F32)


def _dot_tn(a, b):
    return lax.dot_general(a, b, (((0,), (0,)), ((), ())), preferred_element_type=F32)


def _row_block(n, want):
    return want if n % want == 0 else n


def _qkv_body(h_ref, g_ref, w_ref, q_ref, k_ref, v_ref, kb_ref, vb_ref):
    d = h_ref.shape[1]
    xn = _rms(h_ref[...], g_ref[...]).astype(BF)
    q = _dot(xn, w_ref[:, 0:d])
    q_ref[...] = (q * (SB_HEAD_DIM ** -0.5)).astype(BF)
    k = _dot(xn, w_ref[:, d:2 * d])
    k_ref[...] = k
    kb_ref[...] = k.astype(BF)
    v = _dot(xn, w_ref[:, 2 * d:3 * d])
    v_ref[...] = v
    vb_ref[...] = v.astype(BF)


def _qkv(h, g, w):
    n, d = h.shape
    tm = _row_block(n, 512)
    row = pl.BlockSpec((tm, d), lambda i: (i, 0))
    return pl.pallas_call(
        _qkv_body,
        grid=(n // tm,),
        in_specs=[row, pl.BlockSpec((1, d), lambda i: (0, 0)), pl.BlockSpec((d, 3 * d), lambda i: (0, 0))],
        out_specs=[row] * 5,
        out_shape=[jax.ShapeDtypeStruct((n, d), BF), jax.ShapeDtypeStruct((n, d), F32),
                   jax.ShapeDtypeStruct((n, d), F32), jax.ShapeDtypeStruct((n, d), BF),
                   jax.ShapeDtypeStruct((n, d), BF)],
        compiler_params=_params("parallel"),
        name="sb_qkv",
    )(h, g.reshape(1, d), w)


def _strict_upper(n):
    r = lax.broadcasted_iota(jnp.int32, (n, n), 0)
    c = lax.broadcasted_iota(jnp.int32, (n, n), 1)
    return jnp.where(r > c, 1.0, 0.0).astype(BF)


def _sb_tile(z, mask, carry, upper):
    nz = -z
    l1 = jnp.minimum(nz, 0.0) - jnp.log(1.0 + jnp.exp(jnp.minimum(z, nz)))
    if mask is not None:
        l1 = jnp.where(mask, l1, 0.0)
    tail = _dot(l1.astype(BF), upper) + carry
    w = jnp.exp(l1 + z + tail)
    if mask is not None:
        w = jnp.where(mask, w, 0.0)
    return w.astype(BF), carry + jnp.sum(l1, axis=-1, keepdims=True)


def _attn_prompt_body(bias_ref, q_ref, k_ref, v_ref, o_ref, acc_ref, *, tq, tk):
    hp = pl.program_id(1)
    i = pl.program_id(2)
    q = q_ref[0]
    lane = lax.broadcasted_iota(jnp.int32, (1, LANES), 1)
    first = lane < SB_HEAD_DIM
    zero = jnp.zeros_like(q)
    qh = (jnp.where(first, q, zero), jnp.where(first, zero, q))
    bias = (bias_ref[2 * hp], bias_ref[2 * hp + 1])
    upper = _strict_upper(tk)
    acc_ref[...] = jnp.zeros_like(acc_ref)

    def block(kb, carries, masked):
        ks = pl.multiple_of(kb * tk, tk)
        kblk = k_ref[0, pl.ds(ks, tk), :]
        vblk = v_ref[0, pl.ds(ks, tk), :]
        mask = None
        if masked:
            qpos = i * tq + lax.broadcasted_iota(jnp.int32, (tq, tk), 0)
            kpos = ks + lax.broadcasted_iota(jnp.int32, (tq, tk), 1)
            mask = kpos < qpos
        out = []
        for h in range(2):
            z = _dot_nt(qh[h], kblk) + bias[h]
            w, c = _sb_tile(z, mask, carries[h], upper)
            acc_ref[h] += _dot(w, vblk)
            out.append(c)
        return tuple(out)

    nfull = (i * tq) // tk
    ndiag = tq // tk
    carries = (jnp.zeros((tq, 1), F32), jnp.zeros((tq, 1), F32))
    for d in range(ndiag):
        carries = block(nfull + (ndiag - 1 - d), carries, True)
    lax.fori_loop(0, nfull, lambda t, c: block(nfull - 1 - t, c, False), carries)
    o_ref[0] = jnp.where(first, acc_ref[0], acc_ref[1]).astype(o_ref.dtype)


def _attn_prompt(q, k, v, bias, tq=256, tk=256):
    b, t, d = q.shape
    tq = min(tq, t)
    tk = min(tk, tq)
    qspec = pl.BlockSpec((1, tq, LANES), lambda bi, hp, i: (bi, i, hp))
    kspec = pl.BlockSpec((1, t, LANES), lambda bi, hp, i: (bi, 0, hp))
    return pl.pallas_call(
        functools.partial(_attn_prompt_body, tq=tq, tk=tk),
        grid=(b, d // LANES, t // tq),
        in_specs=[pl.BlockSpec(memory_space=pltpu.SMEM), qspec, kspec, kspec],
        out_specs=qspec,
        out_shape=jax.ShapeDtypeStruct((b, t, d), BF),
        scratch_shapes=[pltpu.VMEM((2, tq, LANES), F32)],
        compiler_params=_params("parallel", "parallel", "arbitrary"),
        name="sb_attn_prompt",
    )(bias, q, k, v)


def _attn_sample_body(pt_ref, q_ref, kn_ref, vn_ref, bias_ref, *refs, pg):
    del pt_ref
    k_refs, v_refs = refs[:pg], refs[pg:2 * pg]
    o_ref, acc_ref, carry_ref = refs[2 * pg:]
    g = pl.program_id(1)
    tpad, d = q_ref.shape[1], q_ref.shape[2]
    rows = tpad * SB_HEADS

    head_of_lane = lax.broadcasted_iota(jnp.int32, (SB_HEADS, d), 1) // SB_HEAD_DIM
    own = head_of_lane == lax.broadcasted_iota(jnp.int32, (SB_HEADS, d), 0)
    q = q_ref[0]
    qrows = jnp.concatenate(
        [jnp.where(own, jnp.broadcast_to(q[t:t + 1, :], (SB_HEADS, d)), 0.0) for t in range(tpad)],
        axis=0).astype(BF)
    upper = _strict_upper(PAGE_SIZE)
    bias = bias_ref[...]

    def page(kpage, vpage, mask, carry):
        z = _dot_nt(qrows, kpage.astype(BF)) + bias
        w, carry = _sb_tile(z, mask, carry, upper)
        return _dot(w, vpage.astype(BF)), carry

    @pl.when(g == 0)
    def _():
        pad = jnp.zeros((PAGE_SIZE - tpad, d), F32)
        kn = jnp.concatenate([kn_ref[0], pad], axis=0)
        vn = jnp.concatenate([vn_ref[0], pad], axis=0)
        t_of_row = lax.broadcasted_iota(jnp.int32, (rows, PAGE_SIZE), 0) // SB_HEADS
        mask = lax.broadcasted_iota(jnp.int32, (rows, PAGE_SIZE), 1) < t_of_row
        o, c = page(kn, vn, mask, jnp.zeros((rows, 1), F32))
        acc_ref[...] = o
        carry_ref[...] = c

    carry = carry_ref[...]
    for s in range(pg):
        o, carry = page(k_refs[s][...], v_refs[s][...], None, carry)
        acc_ref[...] += o
    carry_ref[...] = carry

    @pl.when(g == pl.num_programs(1) - 1)
    def _():
        ownf = own.astype(F32)
        o_ref[0] = jnp.concatenate(
            [jnp.sum(acc_ref[t * SB_HEADS:(t + 1) * SB_HEADS, :] * ownf, axis=0, keepdims=True)
             for t in range(tpad)], axis=0)


def _attn_sample(q, k_new, v_new, cache_k, cache_v, layer, page_table, bias, pg=8):
    b, tpad, d = q.shape
    n_pages = page_table.shape[1]
    pg = min(pg, n_pages)
    rows = tpad * SB_HEADS
    bias_tile = jnp.broadcast_to(jnp.tile(bias, tpad)[:, None], (rows, PAGE_SIZE))
    tok = pl.BlockSpec((1, tpad, d), lambda bi, g, pt: (bi, 0, 0))

    def page_spec(s):
        return pl.BlockSpec((None, None, PAGE_SIZE, d),
                            lambda bi, g, pt: (layer, pt[bi, n_pages - 1 - (g * pg + s)], 0, 0))

    specs = [page_spec(s) for s in range(pg)]
    grid_spec = pltpu.PrefetchScalarGridSpec(
        num_scalar_prefetch=1,
        grid=(b, n_pages // pg),
        in_specs=[tok, tok, tok, pl.BlockSpec((rows, PAGE_SIZE), lambda bi, g, pt: (0, 0))] + specs + specs,
        out_specs=tok,
        scratch_shapes=[pltpu.VMEM((rows, d), F32), pltpu.VMEM((rows, 1), F32)],
    )
    return pl.pallas_call(
        functools.partial(_attn_sample_body, pg=pg),
        grid_spec=grid_spec,
        out_shape=jax.ShapeDtypeStruct((b, tpad, d), F32),
        compiler_params=_params("parallel", "arbitrary"),
        name="sb_attn_sample",
    )(page_table, q, k_new, v_new, bias_tile, *([cache_k] * pg), *([cache_v] * pg))


def _mm_res_body(x_ref, w_ref, r_ref, o_ref):
    o_ref[...] = r_ref[...] + _dot(x_ref[...].astype(BF), w_ref[...])


def _mm_res(x, w, res):
    n, kdim = x.shape
    d = w.shape[1]
    tm = _row_block(n, 512)
    return pl.pallas_call(
        _mm_res_body,
        grid=(n // tm,),
        in_specs=[pl.BlockSpec((tm, kdim), lambda i: (i, 0)), pl.BlockSpec((kdim, d), lambda i: (0, 0)),
                  pl.BlockSpec((tm, d), lambda i: (i, 0))],
        out_specs=pl.BlockSpec((tm, d), lambda i: (i, 0)),
        out_shape=jax.ShapeDtypeStruct((n, d), F32),
        compiler_params=_params("parallel"),
        name="mm_res",
    )(x, w, res)


def _gla_in_body(h_ref, g_ref, w_ref, wa_ref, wg_ref, bg_ref, q_ref, k_ref, v_ref, og_ref, la_ref):
    xn = _rms(h_ref[...], g_ref[...]).astype(BF)
    q_ref[...] = _dot(xn, w_ref[:, 0:GLA_DK]) * (GLA_HK ** -0.5)
    k_ref[...] = _dot(xn, w_ref[:, GLA_DK:2 * GLA_DK])
    v_ref[...] = _dot(xn, w_ref[:, 2 * GLA_DK:2 * GLA_DK + GLA_DV])
    og_ref[...] = _dot(xn, w_ref[:, 2 * GLA_DK + GLA_DV:2 * GLA_DK + 2 * GLA_DV])
    a = _dot(xn, wa_ref[...])
    x = _dot(a.astype(BF), wg_ref[...]) + bg_ref[...]
    la_ref[...] = (jnp.minimum(x, 0.0) - jnp.log(1.0 + jnp.exp(-jnp.abs(x)))) / GLA_GATE_TAU


def _gla_in(h, g, w_main, w_a, w_gate, b_gate):
    n, d = h.shape
    tm = _row_block(n, 512)
    nmain = w_main.shape[1]
    const = lambda shape: pl.BlockSpec(shape, lambda i: (0, 0))
    row = lambda width: pl.BlockSpec((tm, width), lambda i: (i, 0))
    return pl.pallas_call(
        _gla_in_body,
        grid=(n // tm,),
        in_specs=[row(d), const((1, d)), const((d, nmain)), const((d, LANES)), const((LANES, GLA_DK)),
                  const((1, GLA_DK))],
        out_specs=[row(GLA_DK), row(GLA_DK), row(GLA_DV), row(GLA_DV), row(GLA_DK)],
        out_shape=[jax.ShapeDtypeStruct((n, GLA_DK), F32), jax.ShapeDtypeStruct((n, GLA_DK), F32),
                   jax.ShapeDtypeStruct((n, GLA_DV), F32), jax.ShapeDtypeStruct((n, GLA_DV), F32),
                   jax.ShapeDtypeStruct((n, GLA_DK), F32)],
        compiler_params=_params("parallel"),
        name="gla_in",
    )(h, g.reshape(1, d), w_main, w_a, w_gate, b_gate.reshape(1, GLA_DK))


def _split3(x):
    hi = x.astype(BF)
    r = x - hi.astype(F32)
    mid = r.astype(BF)
    lo = (r - mid.astype(F32)).astype(BF)
    return hi, mid, lo


def _gla_body(*refs, has_state):
    if has_state:
        q_ref, k_ref, la_ref, v_ref, og_ref, gn_ref, s0_ref, o_ref, s_ref = refs
    else:
        q_ref, k_ref, la_ref, v_ref, og_ref, gn_ref, o_ref, s_ref = refs
    bb, c = q_ref.shape[0], q_ref.shape[1]

    @pl.when(pl.program_id(1) == 0)
    def _():
        s_ref[...] = s0_ref[...] if has_state else jnp.zeros_like(s_ref)

    r = lax.broadcasted_iota(jnp.int32, (c, c), 0)
    col = lax.broadcasted_iota(jnp.int32, (c, c), 1)
    causal = col <= r
    lower = jnp.where(causal, 1.0, 0.0).astype(BF)
    ones = jnp.ones((c, GLA_HK), BF)
    gn = gn_ref[...]

    def one(i, _):
        for h in range(GLA_HEADS):
            ks = slice(h * GLA_HK, (h + 1) * GLA_HK)
            vs = slice(h * GLA_HV, (h + 1) * GLA_HV)
            parts = _split3(la_ref[i, :, ks])
            b = _dot(lower, parts[0]) + _dot(lower, parts[1]) + _dot(lower, parts[2])
            b_col = _dot_tn(parts[0], ones) + _dot_tn(parts[1], ones) + _dot_tn(parts[2], ones)
            b_last = b[c - 1:c, :]
            k = k_ref[i, :, ks]
            qg = (q_ref[i, :, ks] * jnp.exp(b)).astype(BF)
            kg = (k * jnp.exp(-b)).astype(BF)
            kd = (k * jnp.exp(b_last - b)).astype(BF)
            v = v_ref[i, :, vs].astype(BF)
            s = s_ref[i, h]
            a = jnp.where(causal, _dot_nt(qg, kg), 0.0)
            o = _dot(a.astype(BF), v) + _dot(qg, s.astype(BF))
            decay = jnp.exp(b_col)
            s_ref[i, h] = jnp.concatenate([decay] * (GLA_HV // GLA_HK), axis=1) * s + _dot_tn(kd, v)
            og = og_ref[i, :, vs]
            o_ref[i, :, vs] = _rms(o, gn) * (og * jax.nn.sigmoid(og))
        return 0

    if bb == 1:
        one(0, 0)
    else:
        lax.fori_loop(0, bb, one, 0)


def _gla(q, k, la, v, og, gnorm, s0, bb, c):
    b, t, _ = q.shape
    has_state = s0 is not None
    tok = lambda width: pl.BlockSpec((bb, c, width), lambda bi, ci: (bi, ci, 0))
    sspec = pl.BlockSpec((bb, GLA_HEADS, GLA_HK, GLA_HV), lambda bi, ci: (bi, 0, 0, 0))
    in_specs = [tok(GLA_DK), tok(GLA_DK), tok(GLA_DK), tok(GLA_DV), tok(GLA_DV),
                pl.BlockSpec((1, GLA_HV), lambda bi, ci: (0, 0))]
    args = [q, k, la, v, og, gnorm.reshape(1, GLA_HV)]
    if has_state:
        in_specs.append(sspec)
        args.append(s0)
    return pl.pallas_call(
        functools.partial(_gla_body, has_state=has_state),
        grid=(b // bb, t // c),
        in_specs=in_specs,
        out_specs=[tok(GLA_DV), sspec],
        out_shape=[jax.ShapeDtypeStruct((b, t, GLA_DV), F32),
                   jax.ShapeDtypeStruct((b, GLA_HEADS, GLA_HK, GLA_HV), F32)],
        compiler_params=_params("parallel", "arbitrary"),
        name="gla",
    )(*args)


def _ffn_body(h_ref, g_ref, wg_ref, wu_ref, wo_ref, o_ref, xn_ref, acc_ref):
    f = pl.program_id(1)

    @pl.when(f == 0)
    def _():
        xn_ref[...] = _rms(h_ref[...], g_ref[...]).astype(BF)
        acc_ref[...] = jnp.zeros_like(acc_ref)

    xn = xn_ref[...]
    gate = _dot(xn, wg_ref[...])
    up = _dot(xn, wu_ref[...])
    acc_ref[...] += _dot((gate * jax.nn.sigmoid(gate) * up).astype(BF), wo_ref[...])

    @pl.when(f == pl.num_programs(1) - 1)
    def _():
        o_ref[...] = h_ref[...] + acc_ref[...]


def _ffn(h, g, w_in, w_out, tf=1408):
    n, d = h.shape
    dff = w_out.shape[0]
    tm = _row_block(n, 512)
    nf = dff // tf
    row = pl.BlockSpec((tm, d), lambda i, f: (i, 0))
    return pl.pallas_call(
        _ffn_body,
        grid=(n // tm, nf),
        in_specs=[row, pl.BlockSpec((1, d), lambda i, f: (0, 0)),
                  pl.BlockSpec((d, tf), lambda i, f: (0, f)),
                  pl.BlockSpec((d, tf), lambda i, f: (0, f + nf)),
                  pl.BlockSpec((tf, d), lambda i, f: (f, 0))],
        out_specs=row,
        out_shape=jax.ShapeDtypeStruct((n, d), F32),
        scratch_shapes=[pltpu.VMEM((tm, d), BF), pltpu.VMEM((tm, d), F32)],
        compiler_params=_params("parallel", "arbitrary"),
        name="ffn",
    )(h, g.reshape(1, d), w_in, w_in, w_out)


def _ple_body(h_ref, g_ref, wg_ref, p_ref, wp_ref, gf_ref, o_ref, *, final):
    h = h_ref[...]
    gate = jax.nn.sigmoid(_dot(_rms(h, g_ref[...]).astype(BF), wg_ref[...]))
    out = h + gate * _dot(p_ref[...].astype(BF), wp_ref[...])
    o_ref[...] = _rms(out, gf_ref[...]) if final else out


def _ple(h, g, w_gate, p, w_proj, g_final, final):
    n, d = h.shape
    pd = p.shape[1]
    tm = _row_block(n, 512)
    const = lambda shape: pl.BlockSpec(shape, lambda i: (0, 0))
    row = lambda width: pl.BlockSpec((tm, width), lambda i: (i, 0))
    return pl.pallas_call(
        functools.partial(_ple_body, final=final),
        grid=(n // tm,),
        in_specs=[row(d), const((1, d)), const((d, d)), row(pd), const((pd, d)), const((1, d))],
        out_specs=row(d),
        out_shape=jax.ShapeDtypeStruct((n, d), F32),
        compiler_params=_params("parallel"),
        name="ple",
    )(h, g.reshape(1, d), w_gate, p, w_proj, g_final.reshape(1, d))


def _pad_rows(x, rows):
    return jnp.pad(x, ((0, 0), (0, rows - x.shape[1]), (0, 0)))


def _trunk(x, p, w, cache):
    b, t, d = x.shape
    n = b * t
    h = x.reshape(n, d)
    new_k, new_v, new_s = [], [], []
    for i in range(DEPTH):
        j = i // 2
        if i % 2 == 0:
            q, k, v, kb, vb = _qkv(h, w["g_mix"][i], w["w_sb_in"][j])
            new_k.append(k.reshape(b, t, SB_HEADS, SB_HEAD_DIM))
            new_v.append(v.reshape(b, t, SB_HEADS, SB_HEAD_DIM))
            if cache is None:
                o = _attn_prompt(q.reshape(b, t, d), kb.reshape(b, t, d), vb.reshape(b, t, d), w["b_sb"][j])
            else:
                cache_k, cache_v, page_table, _ = cache
                pad = lambda a: _pad_rows(a.reshape(b, t, d).astype(F32), SUBLANES)
                o = _attn_sample(pad(q), pad(k), pad(v), cache_k, cache_v, j, page_table, w["b_sb"][j])[:, :t]
            h = _mm_res(o.reshape(n, d), w["w_sb_out"][j], h)
        else:
            q, k, v, og, la = _gla_in(h, w["g_mix"][i], w["w_gla_main"][j], w["w_gla_a"][j],
                                      w["w_gla_gate"][j], w["b_gla_gate"][j])
            shp = lambda a: a.reshape(b, t, a.shape[-1])
            if cache is None:
                c = GLA_CHUNK if t % GLA_CHUNK == 0 else t
                o, s = _gla(shp(q), shp(k), shp(la), shp(v), shp(og), w["g_gla_norm"][j], None, 1, c)
            else:
                c = 2 * SUBLANES
                pad = lambda a: _pad_rows(shp(a), c)
                o, s = _gla(pad(q), pad(k), pad(la), pad(v), pad(og), w["g_gla_norm"][j],
                            cache[3][j], 8, c)
                o = o[:, :t]
            new_s.append(s)
            h = _mm_res(o.reshape(n, d), w["w_gla_out"][j], h)
        h = _ffn(h, w["g_ffn"][i], w["w_ffn_in"][i], w["w_ffn_out"][i])
        h = _ple(h, w["g_ple"][i], w["w_ple_gate"][i], p[i].reshape(n, -1), w["w_ple_proj"][i],
                 w["g_final"], i == DEPTH - 1)
    return h.reshape(b, t, d), jnp.stack(new_k), jnp.stack(new_v), jnp.stack(new_s)


def kernel(x_prompt, x_sample, cache_sb_k, cache_sb_v, state_gla, page_table, p_prompt, p_sample,
           g_mix, w_sb_in, w_sb_out, b_sb, w_gla_in, w_gla_gate, b_gla_gate, g_gla_norm, w_gla_out,
           g_ffn, w_ffn_in, w_ffn_out, g_ple, w_ple_gate, w_ple_proj, g_final):
    nmain = 2 * GLA_DK + 2 * GLA_DV
    w = dict(
        g_mix=g_mix, b_sb=b_sb, b_gla_gate=b_gla_gate, g_gla_norm=g_gla_norm, g_ffn=g_ffn, g_ple=g_ple,
        g_final=g_final,
        w_sb_in=w_sb_in.astype(BF), w_sb_out=w_sb_out.astype(BF),
        w_gla_main=w_gla_in[:, :, :nmain].astype(BF),
        w_gla_a=jnp.pad(w_gla_in[:, :, nmain:], ((0, 0), (0, 0), (0, LANES - GLA_GATE_RANK))).astype(BF),
        w_gla_gate=jnp.pad(w_gla_gate, ((0, 0), (0, LANES - GLA_GATE_RANK), (0, 0))).astype(BF),
        w_gla_out=w_gla_out.astype(BF), w_ffn_in=w_ffn_in.astype(BF), w_ffn_out=w_ffn_out.astype(BF),
        w_ple_gate=w_ple_gate.astype(BF), w_ple_proj=w_ple_proj.astype(BF),
    )
    n_layers, n_pool = cache_sb_k.shape[0], cache_sb_k.shape[1]
    cache = (cache_sb_k.reshape(n_layers, n_pool, PAGE_SIZE, D_MODEL),
             cache_sb_v.reshape(n_layers, n_pool, PAGE_SIZE, D_MODEL), page_table, state_gla)
    y_prompt, k_prompt, v_prompt, s_prompt = _trunk(x_prompt, p_prompt, w, None)
    y_sample, k_sample, v_sample, s_sample = _trunk(x_sample, p_sample, w, cache)
    return (y_prompt, y_sample, k_prompt, v_prompt, s_prompt, k_sample, v_sample, s_sample)
```

```python
import functools

import jax
import jax.numpy as jnp
from jax import lax
from jax.experimental import pallas as pl
from jax.experimental.pallas import tpu as pltpu

BF = jnp.bfloat16
F32 = jnp.float32

D_MODEL = 1024
DEPTH = 4
PAGE_SIZE = 128
SB_HEADS = 16
SB_HEAD_DIM = D_MODEL // SB_HEADS
GLA_HEADS = 4
GLA_DK = D_MODEL // 2
GLA_DV = D_MODEL
GLA_HK = GLA_DK // GLA_HEADS
GLA_HV = GLA_DV // GLA_HEADS
GLA_GATE_RANK = 16
GLA_GATE_TAU = 16.0
GLA_CHUNK = 64
D_FF = 2816
NORM_EPS = 1e-6

LOG2E = 1.4426950408889634
SB_QSCALE = SB_HEAD_DIM ** -0.5 * LOG2E
BIAS_PARTS = 3
LANES = 128
HEAD_PAD = LANES
SUBLANES = 8
VMEM_LIMIT = 56 * 1024 * 1024


def _params(*sem):
    return pltpu.CompilerParams(dimension_semantics=sem, vmem_limit_bytes=VMEM_LIMIT)


def _rms(x, g):
    return x * lax.rsqrt(jnp.mean(x * x, axis=-1, keepdims=True) + NORM_EPS) * g


def _dot(a, b):
    return jnp.dot(a, b, preferred_element_type=F32)


def _dot_nt(a, b):
    return lax.dot_general(a, b, (((1,), (1,)), ((), ())), preferred_element_type=F32)


def _dot_tn(a, b):
    return lax.dot_general(a, b, (((0,), (0,)), ((), ())), preferred_element_type=F32)


def _row_block(n, want):
    return want if n % want == 0 else n


def _strict_upper(n):
    r = lax.broadcasted_iota(jnp.int32, (n, n), 0)
    c = lax.broadcasted_iota(jnp.int32, (n, n), 1)
    return jnp.where(r > c, 1.0, 0.0).astype(BF)


def _sb_tile(z, mask, carry, upper, newest_first=True):
    blk = upper.shape[0]
    nblk = z.shape[1] // blk
    p = jnp.maximum(z, 0.0) + jnp.log2(1.0 + jnp.exp2(-jnp.abs(z)))
    if mask is not None:
        p = jnp.where(mask, p, 0.0)
    tails = [None] * nblk
    for s in (range(nblk) if newest_first else reversed(range(nblk))):
        ps = p[:, s * blk:(s + 1) * blk]
        tails[s] = _dot(ps.astype(BF), upper) + carry
        carry = carry + jnp.sum(ps, axis=-1, keepdims=True)
    tail = tails[0] if nblk == 1 else jnp.concatenate(tails, axis=1)
    w = jnp.exp2((z - p) - tail)
    if mask is not None:
        w = jnp.where(mask, w, 0.0)
    return w, carry


def _qkv_prompt_body(x_ref, g_ref, wq_ref, wkt_ref, wvt_ref, qfill_ref, qa_ref, kt_ref, vt_ref, kta_ref,
                     vtb_ref):
    xn = _rms(x_ref[0], g_ref[...]).astype(BF)
    qa_ref[0] = (_dot(xn, wq_ref[...]) * SB_QSCALE + qfill_ref[...]).astype(BF)
    kt = _dot_nt(wkt_ref[...], xn)
    kt_ref[0] = kt
    tm = kt.shape[1]
    ones_rows = jnp.where(lax.broadcasted_iota(jnp.int32, (HEAD_PAD - SB_HEAD_DIM, tm), 0) < BIAS_PARTS, 1.0, 0.0)
    pieces = []
    for h in range(SB_HEADS):
        pieces += [kt[h * SB_HEAD_DIM:(h + 1) * SB_HEAD_DIM, :], ones_rows]
    kta_ref[0] = jnp.concatenate(pieces, axis=0).astype(BF)
    vt = _dot_nt(wvt_ref[...], xn)
    vt_ref[0] = vt
    vtb_ref[0] = vt.astype(BF)


def _qkv_prompt(x, g, wq_pad, wkt, wvt, qfill):
    b, t, d = x.shape
    tm = _row_block(t, 512)
    dp = wq_pad.shape[1]
    const = lambda shape: pl.BlockSpec(shape, lambda bi, i: (0, 0))
    tr = lambda rows: pl.BlockSpec((1, rows, tm), lambda bi, i: (bi, 0, i))
    return pl.pallas_call(
        _qkv_prompt_body,
        grid=(b, t // tm),
        in_specs=[pl.BlockSpec((1, tm, d), lambda bi, i: (bi, i, 0)), const((1, d)), const((d, dp)),
                  const((d, d)), const((d, d)), const((1, dp))],
        out_specs=[pl.BlockSpec((1, tm, dp), lambda bi, i: (bi, i, 0)), tr(d), tr(d), tr(dp), tr(d)],
        out_shape=[jax.ShapeDtypeStruct((b, t, dp), BF), jax.ShapeDtypeStruct((b, d, t), F32),
                   jax.ShapeDtypeStruct((b, d, t), F32), jax.ShapeDtypeStruct((b, dp, t), BF),
                   jax.ShapeDtypeStruct((b, d, t), BF)],
        compiler_params=_params("parallel", "parallel"),
        name="sb_qkv_prompt",
    )(x, g.reshape(1, d), wq_pad, wkt, wvt, qfill)


def _qkv_sample_body(h_ref, g_ref, w_ref, q_ref, k_ref, v_ref):
    d = h_ref.shape[1]
    xn = _rms(h_ref[...], g_ref[...]).astype(BF)
    q_ref[...] = _dot(xn, w_ref[:, 0:d]) * SB_QSCALE
    k_ref[...] = _dot(xn, w_ref[:, d:2 * d])
    v_ref[...] = _dot(xn, w_ref[:, 2 * d:3 * d])


def _qkv_sample(h, g, w):
    n, d = h.shape
    tm = _row_block(n, 512)
    row = pl.BlockSpec((tm, d), lambda i: (i, 0))
    return pl.pallas_call(
        _qkv_sample_body,
        grid=(n // tm,),
        in_specs=[row, pl.BlockSpec((1, d), lambda i: (0, 0)), pl.BlockSpec((d, 3 * d), lambda i: (0, 0))],
        out_specs=[row] * 3,
        out_shape=[jax.ShapeDtypeStruct((n, d), F32)] * 3,
        compiler_params=_params("parallel"),
        name="sb_qkv_sample",
    )(h, g.reshape(1, d), w)


def _attn_prompt_body(q_ref, k_ref, v_ref, o_ref, acc_ref, *, tq, tk):
    i = pl.program_id(2)
    upper = _strict_upper(tk)
    acc_ref[...] = jnp.zeros_like(acc_ref)

    def span(kb, nb, carries, masked):
        width = nb * tk
        ks = pl.multiple_of(kb * tk, tk)
        mask = None
        if masked:
            qpos = i * tq + lax.broadcasted_iota(jnp.int32, (tq, width), 0)
            mask = ks + lax.broadcasted_iota(jnp.int32, (tq, width), 1) < qpos
        vblk = v_ref[0, :, pl.ds(ks, width)]
        out = []
        for h in range(2):
            hs = slice(h * HEAD_PAD, (h + 1) * HEAD_PAD)
            z = _dot(q_ref[0, :, hs], k_ref[0, hs, pl.ds(ks, width)])
            w, c = _sb_tile(z, mask, carries[h], upper, newest_first=False)
            acc_ref[h] += _dot_nt(w.astype(BF), vblk)
            out.append(c)
        return tuple(out)

    nfull = (i * tq) // tk
    ndiag = max(tq // tk, 1)
    carries = (jnp.zeros((tq, 1), F32), jnp.zeros((tq, 1), F32))
    carries = span(nfull, ndiag, carries, True)
    carries = lax.fori_loop(0, nfull // 2, lambda t, c: span(nfull - 2 - 2 * t, 2, c, False), carries)
    lax.fori_loop(0, nfull % 2, lambda t, c: span(0, 1, c, False), carries)
    first = lax.broadcasted_iota(jnp.int32, (1, LANES), 1) < SB_HEAD_DIM
    o_ref[0] = jnp.where(first, acc_ref[0], acc_ref[1]).astype(o_ref.dtype)


def _attn_prompt(qa, kta, vtb, tq=512, tk=256):
    b, t, dp = qa.shape
    d = vtb.shape[1]
    tq = min(tq, t)
    tk = min(tk, t)
    assert max(tq, tk) % min(tq, tk) == 0 and t % tq == 0 and t % tk == 0
    return pl.pallas_call(
        functools.partial(_attn_prompt_body, tq=tq, tk=tk),
        grid=(b, d // LANES, t // tq),
        in_specs=[pl.BlockSpec((1, tq, 2 * HEAD_PAD), lambda bi, hp, i: (bi, i, hp)),
                  pl.BlockSpec((1, 2 * HEAD_PAD, t), lambda bi, hp, i: (bi, hp, 0)),
                  pl.BlockSpec((1, LANES, t), lambda bi, hp, i: (bi, hp, 0))],
        out_specs=pl.BlockSpec((1, tq, LANES), lambda bi, hp, i: (bi, i, hp)),
        out_shape=jax.ShapeDtypeStruct((b, t, d), BF),
        scratch_shapes=[pltpu.VMEM((2, tq, LANES), F32)],
        compiler_params=_params("parallel", "parallel", "arbitrary"),
        name="sb_attn_prompt",
    )(qa, kta, vtb)


def _attn_sample_body(pt_ref, q_ref, kn_ref, vn_ref, bias_ref, *refs, pg):
    del pt_ref
    k_refs, v_refs = refs[:pg], refs[pg:2 * pg]
    o_ref, acc_ref, carry_ref = refs[2 * pg:]
    g = pl.program_id(1)
    tpad, d = q_ref.shape[1], q_ref.shape[2]
    rows = tpad * SB_HEADS

    head_of_lane = lax.broadcasted_iota(jnp.int32, (tpad, d), 1) // SB_HEAD_DIM
    q = q_ref[0]
    qrows = jnp.concatenate([jnp.where(head_of_lane == h, q, 0.0) for h in range(SB_HEADS)], axis=0).astype(BF)
    upper = _strict_upper(PAGE_SIZE)
    bias = bias_ref[...] * LOG2E

    @pl.when(g == 0)
    def _():
        pad = jnp.zeros((PAGE_SIZE - tpad, d), F32)
        kn = jnp.concatenate([kn_ref[0], pad], axis=0).astype(BF)
        vn = jnp.concatenate([vn_ref[0], pad], axis=0).astype(BF)
        t_of_row = lax.broadcasted_iota(jnp.int32, (rows, PAGE_SIZE), 0) % tpad
        mask = lax.broadcasted_iota(jnp.int32, (rows, PAGE_SIZE), 1) < t_of_row
        w, c = _sb_tile(_dot_nt(qrows, kn) + bias[:, :PAGE_SIZE], mask, jnp.zeros((rows, 1), F32), upper)
        acc_ref[...] = _dot(w.astype(BF), vn)
        carry_ref[...] = c

    kt = jnp.concatenate([r[...].astype(BF) for r in k_refs], axis=1)
    vt = jnp.concatenate([r[...].astype(BF) for r in v_refs], axis=1)
    w, carry = _sb_tile(_dot(qrows, kt) + bias, None, carry_ref[...], upper)
    acc_ref[...] += _dot_nt(w.astype(BF), vt)
    carry_ref[...] = carry

    @pl.when(g == pl.num_programs(1) - 1)
    def _():
        out = jnp.zeros((tpad, d), F32)
        for h in range(SB_HEADS):
            out = out + jnp.where(head_of_lane == h, acc_ref[h * tpad:(h + 1) * tpad, :], 0.0)
        o_ref[0] = out


def _attn_sample(q, k_new, v_new, cache_kt, cache_vt, layer, page_table, bias, pg=8):
    b, tpad, d = q.shape
    n_pages = page_table.shape[1]
    pg = min(pg, n_pages)
    rows = tpad * SB_HEADS
    bias_tile = jnp.broadcast_to(jnp.repeat(bias, tpad)[:, None], (rows, pg * PAGE_SIZE))
    tok = pl.BlockSpec((1, tpad, d), lambda bi, g, pt: (bi, 0, 0))

    def page_spec(s):
        return pl.BlockSpec((None, None, d, PAGE_SIZE),
                            lambda bi, g, pt: (layer, pt[bi, n_pages - 1 - (g * pg + s)], 0, 0))

    specs = [page_spec(s) for s in range(pg)]
    grid_spec = pltpu.PrefetchScalarGridSpec(
        num_scalar_prefetch=1,
        grid=(b, n_pages // pg),
        in_specs=[tok, tok, tok, pl.BlockSpec((rows, pg * PAGE_SIZE), lambda bi, g, pt: (0, 0))] + specs + specs,
        out_specs=tok,
        scratch_shapes=[pltpu.VMEM((rows, d), F32), pltpu.VMEM((rows, 1), F32)],
    )
    return pl.pallas_call(
        functools.partial(_attn_sample_body, pg=pg),
        grid_spec=grid_spec,
        out_shape=jax.ShapeDtypeStruct((b, tpad, d), F32),
        compiler_params=_params("parallel", "arbitrary"),
        name="sb_attn_sample",
    )(page_table, q, k_new, v_new, bias_tile, *([cache_kt] * pg), *([cache_vt] * pg))


def _mm_res_body(x_ref, w_ref, r_ref, o_ref):
    o_ref[...] = r_ref[...] + _dot(x_ref[...].astype(BF), w_ref[...])


def _mm_res(x, w, res):
    n, kdim = x.shape
    d = w.shape[1]
    tm = _row_block(n, 512)
    return pl.pallas_call(
        _mm_res_body,
        grid=(n // tm,),
        in_specs=[pl.BlockSpec((tm, kdim), lambda i: (i, 0)), pl.BlockSpec((kdim, d), lambda i: (0, 0)),
                  pl.BlockSpec((tm, d), lambda i: (i, 0))],
        out_specs=pl.BlockSpec((tm, d), lambda i: (i, 0)),
        out_shape=jax.ShapeDtypeStruct((n, d), F32),
        compiler_params=_params("parallel"),
        name="mm_res",
    )(x, w, res)


def _gla_in_body(h_ref, g_ref, w_ref, wa_ref, wg_ref, bg_ref, q_ref, k_ref, v_ref, og_ref, la_ref):
    xn = _rms(h_ref[...], g_ref[...]).astype(BF)
    q_ref[...] = _dot(xn, w_ref[:, 0:GLA_DK]) * (GLA_HK ** -0.5)
    k_ref[...] = _dot(xn, w_ref[:, GLA_DK:2 * GLA_DK])
    v_ref[...] = _dot(xn, w_ref[:, 2 * GLA_DK:2 * GLA_DK + GLA_DV])
    og_ref[...] = _dot(xn, w_ref[:, 2 * GLA_DK + GLA_DV:2 * GLA_DK + 2 * GLA_DV])
    a = _dot(xn, wa_ref[...])
    x = _dot(a.astype(BF), wg_ref[...]) + bg_ref[...]
    la_ref[...] = (jnp.minimum(x, 0.0) - jnp.log(1.0 + jnp.exp(-jnp.abs(x)))) / GLA_GATE_TAU


def _gla_in(h, g, w_main, w_a, w_gate, b_gate):
    n, d = h.shape
    tm = _row_block(n, 512)
    nmain = w_main.shape[1]
    const = lambda shape: pl.BlockSpec(shape, lambda i: (0, 0))
    row = lambda width: pl.BlockSpec((tm, width), lambda i: (i, 0))
    return pl.pallas_call(
        _gla_in_body,
        grid=(n // tm,),
        in_specs=[row(d), const((1, d)), const((d, nmain)), const((d, LANES)), const((LANES, GLA_DK)),
                  const((1, GLA_DK))],
        out_specs=[row(GLA_DK), row(GLA_DK), row(GLA_DV), row(GLA_DV), row(GLA_DK)],
        out_shape=[jax.ShapeDtypeStruct((n, GLA_DK), F32), jax.ShapeDtypeStruct((n, GLA_DK), F32),
                   jax.ShapeDtypeStruct((n, GLA_DV), F32), jax.ShapeDtypeStruct((n, GLA_DV), F32),
                   jax.ShapeDtypeStruct((n, GLA_DK), F32)],
        compiler_params=_params("parallel"),
        name="gla_in",
    )(h, g.reshape(1, d), w_main, w_a, w_gate, b_gate.reshape(1, GLA_DK))


def _split3(x):
    hi = x.astype(BF)
    r = x - hi.astype(F32)
    mid = r.astype(BF)
    lo = (r - mid.astype(F32)).astype(BF)
    return hi, mid, lo


def _gla_body(*refs, has_state):
    if has_state:
        q_ref, k_ref, la_ref, v_ref, og_ref, gn_ref, s0_ref, o_ref, s_ref = refs
    else:
        q_ref, k_ref, la_ref, v_ref, og_ref, gn_ref, o_ref, s_ref = refs
    bb, c = q_ref.shape[0], q_ref.shape[1]

    @pl.when(pl.program_id(1) == 0)
    def _():
        s_ref[...] = s0_ref[...] if has_state else jnp.zeros_like(s_ref)

    r = lax.broadcasted_iota(jnp.int32, (c, c), 0)
    col = lax.broadcasted_iota(jnp.int32, (c, c), 1)
    causal = col <= r
    lower = jnp.where(causal, 1.0, 0.0).astype(BF)
    ones = jnp.ones((c, GLA_HK), BF)
    gn = gn_ref[...]

    def one(i, _):
        for h in range(GLA_HEADS):
            ks = slice(h * GLA_HK, (h + 1) * GLA_HK)
            vs = slice(h * GLA_HV, (h + 1) * GLA_HV)
            parts = _split3(la_ref[i, :, ks])
            b = _dot(lower, parts[0]) + _dot(lower, parts[1]) + _dot(lower, parts[2])
            b_col = _dot_tn(parts[0], ones) + _dot_tn(parts[1], ones) + _dot_tn(parts[2], ones)
            b_last = b[c - 1:c, :]
            k = k_ref[i, :, ks]
            qg = (q_ref[i, :, ks] * jnp.exp(b)).astype(BF)
            kg = (k * jnp.exp(-b)).astype(BF)
            kd = (k * jnp.exp(b_last - b)).astype(BF)
            v = v_ref[i, :, vs].astype(BF)
            s = s_ref[i, h]
            a = jnp.where(causal, _dot_nt(qg, kg), 0.0)
            o = _dot(a.astype(BF), v) + _dot(qg, s.astype(BF))
            decay = jnp.exp(b_col)
            s_ref[i, h] = jnp.concatenate([decay] * (GLA_HV // GLA_HK), axis=1) * s + _dot_tn(kd, v)
            og = og_ref[i, :, vs]
            o_ref[i, :, vs] = _rms(o, gn) * (og * jax.nn.sigmoid(og))
        return 0

    if bb == 1:
        one(0, 0)
    else:
        lax.fori_loop(0, bb, one, 0)


def _gla(q, k, la, v, og, gnorm, s0, bb, c):
    b, t, _ = q.shape
    has_state = s0 is not None
    tok = lambda width: pl.BlockSpec((bb, c, width), lambda bi, ci: (bi, ci, 0))
    sspec = pl.BlockSpec((bb, GLA_HEADS, GLA_HK, GLA_HV), lambda bi, ci: (bi, 0, 0, 0))
    in_specs = [tok(GLA_DK), tok(GLA_DK), tok(GLA_DK), tok(GLA_DV), tok(GLA_DV),
                pl.BlockSpec((1, GLA_HV), lambda bi, ci: (0, 0))]
    args = [q, k, la, v, og, gnorm.reshape(1, GLA_HV)]
    if has_state:
        in_specs.append(sspec)
        args.append(s0)
    return pl.pallas_call(
        functools.partial(_gla_body, has_state=has_state),
        grid=(b // bb, t // c),
        in_specs=in_specs,
        out_specs=[tok(GLA_DV), sspec],
        out_shape=[jax.ShapeDtypeStruct((b, t, GLA_DV), F32),
                   jax.ShapeDtypeStruct((b, GLA_HEADS, GLA_HK, GLA_HV), F32)],
        compiler_params=_params("parallel", "arbitrary"),
        name="gla",
    )(*args)


def _ffn_body(h_ref, g_ref, wg_ref, wu_ref, wo_ref, o_ref, xn_ref, acc_ref):
    f = pl.program_id(1)

    @pl.when(f == 0)
    def _():
        xn_ref[...] = _rms(h_ref[...], g_ref[...]).astype(BF)
        acc_ref[...] = jnp.zeros_like(acc_ref)

    xn = xn_ref[...]
    gate = _dot(xn, wg_ref[...])
    up = _dot(xn, wu_ref[...])
    acc_ref[...] += _dot((gate * jax.nn.sigmoid(gate) * up).astype(BF), wo_ref[...])

    @pl.when(f == pl.num_programs(1) - 1)
    def _():
        o_ref[...] = h_ref[...] + acc_ref[...]


def _ffn(h, g, w_in, w_out, tf=1408):
    n, d = h.shape
    dff = w_out.shape[0]
    tm = _row_block(n, 512)
    nf = dff // tf
    row = pl.BlockSpec((tm, d), lambda i, f: (i, 0))
    return pl.pallas_call(
        _ffn_body,
        grid=(n // tm, nf),
        in_specs=[row, pl.BlockSpec((1, d), lambda i, f: (0, 0)),
                  pl.BlockSpec((d, tf), lambda i, f: (0, f)),
                  pl.BlockSpec((d, tf), lambda i, f: (0, f + nf)),
                  pl.BlockSpec((tf, d), lambda i, f: (f, 0))],
        out_specs=row,
        out_shape=jax.ShapeDtypeStruct((n, d), F32),
        scratch_shapes=[pltpu.VMEM((tm, d), BF), pltpu.VMEM((tm, d), F32)],
        compiler_params=_params("parallel", "arbitrary"),
        name="ffn",
    )(h, g.reshape(1, d), w_in, w_in, w_out)


def _ple_body(h_ref, g_ref, wg_ref, p_ref, wp_ref, gf_ref, o_ref, *, final):
    h = h_ref[...]
    gate = jax.nn.sigmoid(_dot(_rms(h, g_ref[...]).astype(BF), wg_ref[...]))
    out = h + gate * _dot(p_ref[...].astype(BF), wp_ref[...])
    o_ref[...] = _rms(out, gf_ref[...]) if final else out


def _ple(h, g, w_gate, p, w_proj, g_final, final):
    n, d = h.shape
    pd = p.shape[1]
    tm = _row_block(n, 512)
    const = lambda shape: pl.BlockSpec(shape, lambda i: (0, 0))
    row = lambda width: pl.BlockSpec((tm, width), lambda i: (i, 0))
    return pl.pallas_call(
        functools.partial(_ple_body, final=final),
        grid=(n // tm,),
        in_specs=[row(d), const((1, d)), const((d, d)), row(pd), const((pd, d)), const((1, d))],
        out_specs=row(d),
        out_shape=jax.ShapeDtypeStruct((n, d), F32),
        compiler_params=_params("parallel"),
        name="ple",
    )(h, g.reshape(1, d), w_gate, p, w_proj, g_final.reshape(1, d))


def _pad_rows(x, rows):
    return jnp.pad(x, ((0, 0), (0, rows - x.shape[1]), (0, 0)))


def _trunk(x, p, w, cache):
    b, t, d = x.shape
    n = b * t
    h = x.reshape(n, d)
    new_k, new_v, new_s = [], [], []
    for i in range(DEPTH):
        j = i // 2
        if i % 2 == 0:
            if cache is None:
                qa, kt, vt, kta, vtb = _qkv_prompt(h.reshape(b, t, d), w["g_mix"][i], w["w_sb_q_pad"][j],
                                                   w["w_sb_kt"][j], w["w_sb_vt"][j], w["sb_qfill"][j])
                new_k.append(kt)
                new_v.append(vt)
                o = _attn_prompt(qa, kta, vtb)
            else:
                cache_kt, cache_vt, page_table, _ = cache
                q, k, v = _qkv_sample(h, w["g_mix"][i], w["w_sb_in"][j])
                new_k.append(k.reshape(b, t, d))
                new_v.append(v.reshape(b, t, d))
                pad = lambda a: _pad_rows(a.reshape(b, t, d), SUBLANES)
                o = _attn_sample(pad(q), pad(k), pad(v), cache_kt, cache_vt, j, page_table, w["b_sb"][j])[:, :t]
            h = _mm_res(o.reshape(n, d), w["w_sb_out"][j], h)
        else:
            q, k, v, og, la = _gla_in(h, w["g_mix"][i], w["w_gla_main"][j], w["w_gla_a"][j],
                                      w["w_gla_gate"][j], w["b_gla_gate"][j])
            shp = lambda a: a.reshape(b, t, a.shape[-1])
            if cache is None:
                c = GLA_CHUNK if t % GLA_CHUNK == 0 else t
                o, s = _gla(shp(q), shp(k), shp(la), shp(v), shp(og), w["g_gla_norm"][j], None, 1, c)
            else:
                c = 2 * SUBLANES
                pad = lambda a: _pad_rows(shp(a), c)
                o, s = _gla(pad(q), pad(k), pad(la), pad(v), pad(og), w["g_gla_norm"][j],
                            cache[3][j], 8, c)
                o = o[:, :t]
            new_s.append(s)
            h = _mm_res(o.reshape(n, d), w["w_gla_out"][j], h)
        h = _ffn(h, w["g_ffn"][i], w["w_ffn_in"][i], w["w_ffn_out"][i])
        h = _ple(h, w["g_ple"][i], w["w_ple_gate"][i], p[i].reshape(n, -1), w["w_ple_proj"][i],
                 w["g_final"], i == DEPTH - 1)
    return h.reshape(b, t, d), jnp.stack(new_k), jnp.stack(new_v), jnp.stack(new_s)


def _split_bias(bias):
    rest = bias * LOG2E
    parts = []
    for _ in range(BIAS_PARTS):
        part = rest.astype(BF).astype(F32)
        parts.append(part)
        rest = rest - part
    fill = jnp.stack(parts, axis=-1)
    fill = jnp.pad(fill, ((0, 0), (0, 0), (SB_HEAD_DIM, HEAD_PAD - SB_HEAD_DIM - BIAS_PARTS)))
    return fill.reshape(bias.shape[0], 1, SB_HEADS * HEAD_PAD)


def kernel(x_prompt, x_sample, cache_sb_k, cache_sb_v, state_gla, page_table, p_prompt, p_sample,
           g_mix, w_sb_in, w_sb_out, b_sb, w_gla_in, w_gla_gate, b_gla_gate, g_gla_norm, w_gla_out,
           g_ffn, w_ffn_in, w_ffn_out, g_ple, w_ple_gate, w_ple_proj, g_final):
    nmain = 2 * GLA_DK + 2 * GLA_DV
    n_sb = w_sb_in.shape[0]
    w_sb_bf = w_sb_in.astype(BF)
    wq_heads = w_sb_bf[:, :, :D_MODEL].reshape(n_sb, D_MODEL, SB_HEADS, SB_HEAD_DIM)
    w = dict(
        g_mix=g_mix, b_sb=b_sb, b_gla_gate=b_gla_gate, g_gla_norm=g_gla_norm, g_ffn=g_ffn, g_ple=g_ple,
        g_final=g_final,
        w_sb_in=w_sb_bf, w_sb_out=w_sb_out.astype(BF),
        w_sb_q_pad=jnp.pad(wq_heads, ((0, 0), (0, 0), (0, 0), (0, HEAD_PAD - SB_HEAD_DIM))).reshape(
            n_sb, D_MODEL, SB_HEADS * HEAD_PAD),
        w_sb_kt=jnp.swapaxes(w_sb_bf[:, :, D_MODEL:2 * D_MODEL], 1, 2),
        w_sb_vt=jnp.swapaxes(w_sb_bf[:, :, 2 * D_MODEL:], 1, 2),
        sb_qfill=_split_bias(b_sb),
        w_gla_main=w_gla_in[:, :, :nmain].astype(BF),
        w_gla_a=jnp.pad(w_gla_in[:, :, nmain:], ((0, 0), (0, 0), (0, LANES - GLA_GATE_RANK))).astype(BF),
        w_gla_gate=jnp.pad(w_gla_gate, ((0, 0), (0, LANES - GLA_GATE_RANK), (0, 0))).astype(BF),
        w_gla_out=w_gla_out.astype(BF), w_ffn_in=w_ffn_in.astype(BF), w_ffn_out=w_ffn_out.astype(BF),
        w_ple_gate=w_ple_gate.astype(BF), w_ple_proj=w_ple_proj.astype(BF),
    )
    n_layers, n_pool = cache_sb_k.shape[0], cache_sb_k.shape[1]
    kt_view = lambda c: jnp.transpose(c, (0, 1, 3, 4, 2)).reshape(n_layers, n_pool, D_MODEL, PAGE_SIZE)
    cache = (kt_view(cache_sb_k), kt_view(cache_sb_v), page_table, state_gla)
    y_prompt, kt_prompt, vt_prompt, s_prompt = _trunk(x_prompt, p_prompt, w, None)
    y_sample, k_sample, v_sample, s_sample = _trunk(x_sample, p_sample, w, cache)
    bp, tp = x_prompt.shape[0], x_prompt.shape[1]
    bs, ts = x_sample.shape[0], x_sample.shape[1]
    heads_t = lambda a: jnp.transpose(a.reshape(n_sb, bp, SB_HEADS, SB_HEAD_DIM, tp), (0, 1, 4, 2, 3))
    heads = lambda a: a.reshape(n_sb, bs, ts, SB_HEADS, SB_HEAD_DIM)
    return (y_prompt, y_sample, heads_t(kt_prompt), heads_t(vt_prompt), s_prompt,
            heads(k_sample), heads(v_sample), s_sample)
```

```python
import functools

import jax
import jax.numpy as jnp
from jax import lax
from jax.experimental import pallas as pl
from jax.experimental.pallas import tpu as pltpu

BF = jnp.bfloat16
F32 = jnp.float32

D_MODEL = 1024
DEPTH = 4
PAGE_SIZE = 128
SB_HEADS = 16
SB_HEAD_DIM = D_MODEL // SB_HEADS
GLA_HEADS = 4
GLA_DK = D_MODEL // 2
GLA_DV = D_MODEL
GLA_HK = GLA_DK // GLA_HEADS
GLA_HV = GLA_DV // GLA_HEADS
GLA_GATE_RANK = 16
GLA_GATE_TAU = 16.0
GLA_CHUNK = 64
GLA_CHUNKS_PER_STEP = 4
D_FF = 2816
NORM_EPS = 1e-6

LOG2E = 1.4426950408889634
SB_QSCALE = SB_HEAD_DIM ** -0.5 * LOG2E
BIAS_PARTS = 3
LANES = 128
HEAD_PAD = LANES
SUBLANES = 8
VMEM_LIMIT = 56 * 1024 * 1024


def _params(*sem):
    return pltpu.CompilerParams(dimension_semantics=sem, vmem_limit_bytes=VMEM_LIMIT)


def _rms(x, g):
    return x * lax.rsqrt(jnp.mean(x * x, axis=-1, keepdims=True) + NORM_EPS) * g


def _dot(a, b):
    return jnp.dot(a, b, preferred_element_type=F32)


def _dot_nt(a, b):
    return lax.dot_general(a, b, (((1,), (1,)), ((), ())), preferred_element_type=F32)


def _dot_tn(a, b):
    return lax.dot_general(a, b, (((0,), (0,)), ((), ())), preferred_element_type=F32)


def _row_block(n, want):
    return want if n % want == 0 else n


def _tail_matrix(n):
    r = lax.broadcasted_iota(jnp.int32, (n, n), 0)
    c = lax.broadcasted_iota(jnp.int32, (n, n), 1)
    return jnp.where(r > c, 1.0, 0.0).astype(BF)


def _sb_tile(z, mask, carry, tail_matrix):
    blk = tail_matrix.shape[0]
    p = jnp.maximum(z, 0.0) + jnp.log2(1.0 + jnp.exp2(-jnp.abs(z)))
    if mask is not None:
        p = jnp.where(mask, p, 0.0)
    nblk = z.shape[1] // blk
    tails = []
    for s in range(nblk):
        ps = p[:, s * blk:(s + 1) * blk]
        tails.append(_dot(ps.astype(BF), tail_matrix) + jnp.tile(carry, (1, blk // LANES)))
        if s < nblk - 1:
            carry = carry + jnp.sum(ps, axis=-1, keepdims=True)
    tail = tails[0] if nblk == 1 else jnp.concatenate(tails, axis=1)
    w = jnp.exp2((z - p) - tail)
    if mask is not None:
        w = jnp.where(mask, w, 0.0)
    return w, carry + jnp.sum(p[:, (nblk - 1) * blk:], axis=-1, keepdims=True)


def _qkv_prompt_body(x_ref, g_ref, wq_ref, wkt_ref, wvt_ref, qfill_ref, qa_ref, kt_ref, vt_ref, kta_ref,
                     vtb_ref):
    xn = _rms(x_ref[0], g_ref[...]).astype(BF)
    qa_ref[0] = (_dot(xn, wq_ref[...]) * SB_QSCALE + qfill_ref[...]).astype(BF)
    kt = _dot_nt(wkt_ref[...], xn)
    kt_ref[0] = kt
    tm = kt.shape[1]
    ones_rows = jnp.where(lax.broadcasted_iota(jnp.int32, (HEAD_PAD - SB_HEAD_DIM, tm), 0) < BIAS_PARTS, 1.0, 0.0)
    pieces = []
    for h in range(SB_HEADS):
        pieces += [kt[h * SB_HEAD_DIM:(h + 1) * SB_HEAD_DIM, :], ones_rows]
    kta_ref[0] = jnp.concatenate(pieces, axis=0).astype(BF)
    vt = _dot_nt(wvt_ref[...], xn)
    vt_ref[0] = vt
    vtb_ref[0] = vt.astype(BF)


def _qkv_prompt(x, g, wq_pad, wkt, wvt, qfill, layer, sb_layer):
    b, t, d = x.shape
    tm = _row_block(t, 512)
    dp = wq_pad.shape[2]
    const = lambda shape, l: pl.BlockSpec((None,) + shape, lambda bi, i: (l, 0, 0))
    tr = lambda rows: pl.BlockSpec((1, rows, tm), lambda bi, i: (bi, 0, i))
    return pl.pallas_call(
        _qkv_prompt_body,
        grid=(b, t // tm),
        in_specs=[pl.BlockSpec((1, tm, d), lambda bi, i: (bi, i, 0)), const((1, d), layer),
                  const((d, dp), sb_layer), const((d, d), sb_layer), const((d, d), sb_layer),
                  const((1, dp), sb_layer)],
        out_specs=[pl.BlockSpec((1, tm, dp), lambda bi, i: (bi, i, 0)), tr(d), tr(d), tr(dp), tr(d)],
        out_shape=[jax.ShapeDtypeStruct((b, t, dp), BF), jax.ShapeDtypeStruct((b, d, t), F32),
                   jax.ShapeDtypeStruct((b, d, t), F32), jax.ShapeDtypeStruct((b, dp, t), BF),
                   jax.ShapeDtypeStruct((b, d, t), BF)],
        compiler_params=_params("parallel", "parallel"),
        name="sb_qkv_prompt",
    )(x, g, wq_pad, wkt, wvt, qfill)


def _qkv_sample_body(h_ref, g_ref, w_ref, q_ref, k_ref, v_ref):
    d = h_ref.shape[1]
    xn = _rms(h_ref[...], g_ref[...]).astype(BF)
    q_ref[...] = _dot(xn, w_ref[:, 0:d]) * SB_QSCALE
    k_ref[...] = _dot(xn, w_ref[:, d:2 * d])
    v_ref[...] = _dot(xn, w_ref[:, 2 * d:3 * d])


def _qkv_sample(h, g, w, layer, sb_layer):
    n, d = h.shape
    tm = _row_block(n, 512)
    row = pl.BlockSpec((tm, d), lambda i: (i, 0))
    const = lambda shape, l: pl.BlockSpec((None,) + shape, lambda i: (l, 0, 0))
    return pl.pallas_call(
        _qkv_sample_body,
        grid=(n // tm,),
        in_specs=[row, const((1, d), layer), const((d, 3 * d), sb_layer)],
        out_specs=[row] * 3,
        out_shape=[jax.ShapeDtypeStruct((n, d), F32)] * 3,
        compiler_params=_params("parallel"),
        name="sb_qkv_sample",
    )(h, g, w)


def _attn_prompt_body(q_ref, k_ref, v_ref, o_ref, acc_ref, *, tq, tk):
    i = pl.program_id(2)
    tail_matrix = _tail_matrix(tk)
    acc_ref[...] = jnp.zeros_like(acc_ref)

    def block(kb, carries, masked, row0=0):
        rows = tq - row0
        ks = pl.multiple_of(kb * tk, tk)
        mask = None
        if masked:
            qpos = i * tq + row0 + lax.broadcasted_iota(jnp.int32, (rows, tk), 0)
            mask = ks + lax.broadcasted_iota(jnp.int32, (rows, tk), 1) < qpos
        vblk = v_ref[0, :, pl.ds(ks, tk)]
        out = []
        for h in range(2):
            hs = slice(h * HEAD_PAD, (h + 1) * HEAD_PAD)
            z = _dot(q_ref[0, row0:, hs], k_ref[0, hs, pl.ds(ks, tk)])
            w, c = _sb_tile(z, mask, carries[h][row0:], tail_matrix)
            acc_ref[h, row0:, :] += _dot_nt(w.astype(BF), vblk)
            out.append(c if row0 == 0 else jnp.concatenate([carries[h][:row0], c], axis=0))
        return tuple(out)

    nfull = (i * tq) // tk
    ndiag = max(tq // tk, 1)
    carries = (jnp.zeros((tq, LANES), F32), jnp.zeros((tq, LANES), F32))
    for c in reversed(range(ndiag)):
        carries = block(nfull + c, carries, True, row0=c * tk if tq > tk else 0)
    carries = lax.fori_loop(0, nfull // 2,
                            lambda t, c: block(nfull - 2 - 2 * t, block(nfull - 1 - 2 * t, c, False), False), carries)
    lax.fori_loop(0, nfull % 2, lambda t, c: block(0, c, False), carries)
    first = lax.broadcasted_iota(jnp.int32, (1, LANES), 1) < SB_HEAD_DIM
    o_ref[0] = jnp.where(first, acc_ref[0], acc_ref[1]).astype(o_ref.dtype)


def _attn_prompt(qa, kta, vtb, tq=1024, tk=256):
    b, t, dp = qa.shape
    d = vtb.shape[1]
    tq = min(tq, t)
    tk = min(tk, t)
    assert max(tq, tk) % min(tq, tk) == 0 and t % tq == 0 and t % tk == 0
    return pl.pallas_call(
        functools.partial(_attn_prompt_body, tq=tq, tk=tk),
        grid=(b, d // LANES, t // tq),
        in_specs=[pl.BlockSpec((1, tq, 2 * HEAD_PAD), lambda bi, hp, i: (bi, i, hp)),
                  pl.BlockSpec((1, 2 * HEAD_PAD, t), lambda bi, hp, i: (bi, hp, 0)),
                  pl.BlockSpec((1, LANES, t), lambda bi, hp, i: (bi, hp, 0))],
        out_specs=pl.BlockSpec((1, tq, LANES), lambda bi, hp, i: (bi, i, hp)),
        out_shape=jax.ShapeDtypeStruct((b, t, d), BF),
        scratch_shapes=[pltpu.VMEM((2, tq, LANES), F32)],
        compiler_params=_params("parallel", "parallel", "arbitrary"),
        name="sb_attn_prompt",
    )(qa, kta, vtb)


def _attn_sample_body(pt_ref, q_ref, kn_ref, vn_ref, bias_ref, *refs, pg):
    del pt_ref
    k_refs, v_refs = refs[:pg], refs[pg:2 * pg]
    o_ref, acc_ref, carry_ref = refs[2 * pg:]
    g = pl.program_id(1)
    tpad, d = q_ref.shape[1], q_ref.shape[2]
    rows = tpad * SB_HEADS

    head_of_lane = lax.broadcasted_iota(jnp.int32, (tpad, d), 1) // SB_HEAD_DIM
    q = q_ref[0]
    qrows = jnp.concatenate([jnp.where(head_of_lane == h, q, 0.0) for h in range(SB_HEADS)], axis=0).astype(BF)
    tail_matrix = _tail_matrix(PAGE_SIZE)
    bias = bias_ref[...] * LOG2E

    @pl.when(g == 0)
    def _():
        pad = jnp.zeros((PAGE_SIZE - tpad, d), F32)
        kn = jnp.concatenate([kn_ref[0], pad], axis=0).astype(BF)
        vn = jnp.concatenate([vn_ref[0], pad], axis=0).astype(BF)
        t_of_row = lax.broadcasted_iota(jnp.int32, (rows, PAGE_SIZE), 0) % tpad
        mask = lax.broadcasted_iota(jnp.int32, (rows, PAGE_SIZE), 1) < t_of_row
        w, c = _sb_tile(_dot_nt(qrows, kn) + bias[:, :PAGE_SIZE], mask, jnp.zeros((rows, LANES), F32), tail_matrix)
        acc_ref[...] = _dot(w.astype(BF), vn)
        carry_ref[...] = c

    kt = jnp.concatenate([r[...].astype(BF) for r in k_refs], axis=1)
    vt = jnp.concatenate([r[...].astype(BF) for r in v_refs], axis=1)
    w, carry = _sb_tile(_dot(qrows, kt) + bias, None, carry_ref[...], tail_matrix)
    acc_ref[...] += _dot_nt(w.astype(BF), vt)
    carry_ref[...] = carry

    @pl.when(g == pl.num_programs(1) - 1)
    def _():
        out = jnp.zeros((tpad, d), F32)
        for h in range(SB_HEADS):
            out = out + jnp.where(head_of_lane == h, acc_ref[h * tpad:(h + 1) * tpad, :], 0.0)
        o_ref[0] = out


def _attn_sample(q, k_new, v_new, cache_kt, cache_vt, layer, page_table, bias, pg=8):
    b, tpad, d = q.shape
    n_pages = page_table.shape[1]
    pg = min(pg, n_pages)
    rows = tpad * SB_HEADS
    bias_tile = jnp.broadcast_to(jnp.repeat(bias, tpad)[:, None], (rows, pg * PAGE_SIZE))
    tok = pl.BlockSpec((1, tpad, d), lambda bi, g, pt: (bi, 0, 0))

    def page_spec(s):
        return pl.BlockSpec((None, None, d, PAGE_SIZE),
                            lambda bi, g, pt: (layer, pt[bi, n_pages - 1 - (g * pg + s)], 0, 0))

    specs = [page_spec(s) for s in range(pg)]
    grid_spec = pltpu.PrefetchScalarGridSpec(
        num_scalar_prefetch=1,
        grid=(b, n_pages // pg),
        in_specs=[tok, tok, tok, pl.BlockSpec((rows, pg * PAGE_SIZE), lambda bi, g, pt: (0, 0))] + specs + specs,
        out_specs=tok,
        scratch_shapes=[pltpu.VMEM((rows, d), F32), pltpu.VMEM((rows, LANES), F32)],
    )
    return pl.pallas_call(
        functools.partial(_attn_sample_body, pg=pg),
        grid_spec=grid_spec,
        out_shape=jax.ShapeDtypeStruct((b, tpad, d), F32),
        compiler_params=_params("parallel", "arbitrary"),
        name="sb_attn_sample",
    )(page_table, q, k_new, v_new, bias_tile, *([cache_kt] * pg), *([cache_vt] * pg))


def _mm_res_body(x_ref, w_ref, r_ref, o_ref):
    o_ref[...] = r_ref[...] + _dot(x_ref[...].astype(BF), w_ref[...])


def _mm_res(x, w, layer, res):
    n, kdim = x.shape
    d = w.shape[2]
    tm = _row_block(n, 512)
    return pl.pallas_call(
        _mm_res_body,
        grid=(n // tm,),
        in_specs=[pl.BlockSpec((tm, kdim), lambda i: (i, 0)),
                  pl.BlockSpec((None, kdim, d), lambda i: (layer, 0, 0)),
                  pl.BlockSpec((tm, d), lambda i: (i, 0))],
        out_specs=pl.BlockSpec((tm, d), lambda i: (i, 0)),
        out_shape=jax.ShapeDtypeStruct((n, d), F32),
        compiler_params=_params("parallel"),
        name="mm_res",
    )(x, w, res)


def _gla_in_body(h_ref, g_ref, w_ref, wa_ref, wg_ref, bg_ref, q_ref, k_ref, v_ref, og_ref, la_ref):
    xn = _rms(h_ref[...], g_ref[...]).astype(BF)
    q_ref[...] = _dot(xn, w_ref[:, 0:GLA_DK]) * (GLA_HK ** -0.5)
    k_ref[...] = _dot(xn, w_ref[:, GLA_DK:2 * GLA_DK])
    v_ref[...] = _dot(xn, w_ref[:, 2 * GLA_DK:2 * GLA_DK + GLA_DV])
    og_ref[...] = _dot(xn, w_ref[:, 2 * GLA_DK + GLA_DV:2 * GLA_DK + 2 * GLA_DV])
    a = _dot(xn, wa_ref[...])
    x = _dot(a.astype(BF), wg_ref[...]) + bg_ref[...]
    la_ref[...] = (jnp.minimum(x, 0.0) - jnp.log(1.0 + jnp.exp(-jnp.abs(x)))) / GLA_GATE_TAU


def _gla_in(h, g, w_main, w_a, w_gate, b_gate, layer, gla_layer):
    n, d = h.shape
    tm = _row_block(n, 512)
    nmain = w_main.shape[2]
    const = lambda shape, l: pl.BlockSpec((None,) + shape, lambda i: (l, 0, 0))
    row = lambda width: pl.BlockSpec((tm, width), lambda i: (i, 0))
    return pl.pallas_call(
        _gla_in_body,
        grid=(n // tm,),
        in_specs=[row(d), const((1, d), layer), const((d, nmain), gla_layer), const((d, LANES), gla_layer),
                  const((LANES, GLA_DK), gla_layer), const((1, GLA_DK), gla_layer)],
        out_specs=[row(GLA_DK), row(GLA_DK), row(GLA_DV), row(GLA_DV), row(GLA_DK)],
        out_shape=[jax.ShapeDtypeStruct((n, GLA_DK), F32), jax.ShapeDtypeStruct((n, GLA_DK), F32),
                   jax.ShapeDtypeStruct((n, GLA_DV), F32), jax.ShapeDtypeStruct((n, GLA_DV), F32),
                   jax.ShapeDtypeStruct((n, GLA_DK), F32)],
        compiler_params=_params("parallel"),
        name="gla_in",
    )(h, g, w_main, w_a, w_gate, b_gate)


def _split3(x):
    hi = x.astype(BF)
    r = x - hi.astype(F32)
    mid = r.astype(BF)
    lo = (r - mid.astype(F32)).astype(BF)
    return hi, mid, lo


def _gla_body(*refs, has_state, c):
    if has_state:
        q_ref, k_ref, la_ref, v_ref, og_ref, gn_ref, s0_ref, o_ref, s_ref = refs
    else:
        q_ref, k_ref, la_ref, v_ref, og_ref, gn_ref, o_ref, s_ref = refs
    bb, nsub = q_ref.shape[0], q_ref.shape[1] // c

    @pl.when(pl.program_id(1) == 0)
    def _():
        s_ref[...] = s0_ref[...] if has_state else jnp.zeros_like(s_ref)

    r = lax.broadcasted_iota(jnp.int32, (c, c), 0)
    col = lax.broadcasted_iota(jnp.int32, (c, c), 1)
    causal = col <= r
    lower = jnp.where(causal, 1.0, 0.0).astype(BF)
    ones = jnp.ones((c, GLA_HK), BF)
    gn = gn_ref[...]

    def one(i, _):
        state = [s_ref[i, h] for h in range(GLA_HEADS)]
        for j in range(nsub):
            rows = slice(j * c, (j + 1) * c)
            parts = _split3(la_ref[i, rows, :])
            b = _dot(lower, parts[0]) + _dot(lower, parts[1]) + _dot(lower, parts[2])
            b_col = _dot_tn(parts[0], ones) + _dot_tn(parts[1], ones) + _dot_tn(parts[2], ones)
            b_last = b[c - 1:c, :]
            k = k_ref[i, rows, :]
            qg = (q_ref[i, rows, :] * jnp.exp(b)).astype(BF)
            kg = (k * jnp.exp(-b)).astype(BF)
            kd = (k * jnp.exp(b_last - b)).astype(BF)
            decay = jnp.exp(b_col)
            for h in range(GLA_HEADS):
                ks = slice(h * GLA_HK, (h + 1) * GLA_HK)
                vs = slice(h * GLA_HV, (h + 1) * GLA_HV)
                v = v_ref[i, rows, vs].astype(BF)
                a = jnp.where(causal, _dot_nt(qg[:, ks], kg[:, ks]), 0.0)
                o = _dot(a.astype(BF), v) + _dot(qg[:, ks], state[h].astype(BF))
                state[h] = (jnp.concatenate([decay[ks, :]] * (GLA_HV // GLA_HK), axis=1) * state[h]
                            + _dot_tn(kd[:, ks], v))
                og = og_ref[i, rows, vs]
                o_ref[i, rows, vs] = _rms(o, gn) * (og * jax.nn.sigmoid(og))
        for h in range(GLA_HEADS):
            s_ref[i, h] = state[h]
        return 0

    if bb == 1:
        one(0, 0)
    else:
        lax.fori_loop(0, bb, one, 0)


def _gla(q, k, la, v, og, gnorm, s0, layer, bb, c, nsub):
    b, t, _ = q.shape
    has_state = s0 is not None
    tok = lambda width: pl.BlockSpec((bb, c * nsub, width), lambda bi, ci: (bi, ci, 0))
    sspec = pl.BlockSpec((bb, GLA_HEADS, GLA_HK, GLA_HV), lambda bi, ci: (bi, 0, 0, 0))
    in_specs = [tok(GLA_DK), tok(GLA_DK), tok(GLA_DK), tok(GLA_DV), tok(GLA_DV),
                pl.BlockSpec((None, 1, GLA_HV), lambda bi, ci: (layer, 0, 0))]
    args = [q, k, la, v, og, gnorm]
    if has_state:
        in_specs.append(pl.BlockSpec((None, bb, GLA_HEADS, GLA_HK, GLA_HV), lambda bi, ci: (layer, bi, 0, 0, 0)))
        args.append(s0)
    return pl.pallas_call(
        functools.partial(_gla_body, has_state=has_state, c=c),
        grid=(b // bb, t // (c * nsub)),
        in_specs=in_specs,
        out_specs=[tok(GLA_DV), sspec],
        out_shape=[jax.ShapeDtypeStruct((b, t, GLA_DV), F32),
                   jax.ShapeDtypeStruct((b, GLA_HEADS, GLA_HK, GLA_HV), F32)],
        compiler_params=_params("parallel", "arbitrary"),
        name="gla",
    )(*args)


def _ffn_body(h_ref, g_ref, wg_ref, wu_ref, wo_ref, o_ref, xn_ref, acc_ref):
    f = pl.program_id(1)

    @pl.when(f == 0)
    def _():
        xn_ref[...] = _rms(h_ref[...], g_ref[...]).astype(BF)
        acc_ref[...] = jnp.zeros_like(acc_ref)

    xn = xn_ref[...]
    gate = _dot(xn, wg_ref[...])
    up = _dot(xn, wu_ref[...])
    acc_ref[...] += _dot((gate * jax.nn.sigmoid(gate) * up).astype(BF), wo_ref[...])

    @pl.when(f == pl.num_programs(1) - 1)
    def _():
        o_ref[...] = h_ref[...] + acc_ref[...]


def _ffn(h, g, w_in, w_out, layer, tf=1408):
    n, d = h.shape
    dff = w_out.shape[1]
    tm = _row_block(n, 512)
    nf = dff // tf
    row = pl.BlockSpec((tm, d), lambda i, f: (i, 0))
    return pl.pallas_call(
        _ffn_body,
        grid=(n // tm, nf),
        in_specs=[row, pl.BlockSpec((None, 1, d), lambda i, f: (layer, 0, 0)),
                  pl.BlockSpec((None, d, tf), lambda i, f: (layer, 0, f)),
                  pl.BlockSpec((None, d, tf), lambda i, f: (layer, 0, f + nf)),
                  pl.BlockSpec((None, tf, d), lambda i, f: (layer, f, 0))],
        out_specs=row,
        out_shape=jax.ShapeDtypeStruct((n, d), F32),
        scratch_shapes=[pltpu.VMEM((tm, d), BF), pltpu.VMEM((tm, d), F32)],
        compiler_params=_params("parallel", "arbitrary"),
        name="ffn",
    )(h, g, w_in, w_in, w_out)


def _ple_body(h_ref, g_ref, wg_ref, p_ref, wp_ref, gf_ref, o_ref, *, final):
    h = h_ref[...]
    gate = jax.nn.sigmoid(_dot(_rms(h, g_ref[...]).astype(BF), wg_ref[...]))
    out = h + gate * _dot(p_ref[...].astype(BF), wp_ref[...])
    o_ref[...] = _rms(out, gf_ref[...]) if final else out


def _ple(h, g, w_gate, p, w_proj, g_final, layer, final):
    n, d = h.shape
    pd = p.shape[2]
    tm = _row_block(n, 512)
    const = lambda shape: pl.BlockSpec((None,) + shape, lambda i: (layer, 0, 0))
    row = lambda width: pl.BlockSpec((tm, width), lambda i: (i, 0))
    return pl.pallas_call(
        functools.partial(_ple_body, final=final),
        grid=(n // tm,),
        in_specs=[row(d), const((1, d)), const((d, d)), pl.BlockSpec((None, tm, pd), lambda i: (layer, i, 0)),
                  const((pd, d)), pl.BlockSpec((1, d), lambda i: (0, 0))],
        out_specs=row(d),
        out_shape=jax.ShapeDtypeStruct((n, d), F32),
        compiler_params=_params("parallel"),
        name="ple",
    )(h, g, w_gate, p, w_proj, g_final.reshape(1, d))


def _pad_rows(x, rows):
    return jnp.pad(x, ((0, 0), (0, rows - x.shape[1]), (0, 0)))


def _trunk(x, p, w, cache):
    b, t, d = x.shape
    n = b * t
    h = x.reshape(n, d)
    p = p.reshape(DEPTH, n, -1)
    new_k, new_v, new_s = [], [], []
    for i in range(DEPTH):
        j = i // 2
        if i % 2 == 0:
            if cache is None:
                qa, kt, vt, kta, vtb = _qkv_prompt(h.reshape(b, t, d), w["g_mix"], w["w_sb_q_pad"], w["w_sb_kt"],
                                                   w["w_sb_vt"], w["sb_qfill"], i, j)
                new_k.append(kt)
                new_v.append(vt)
                o = _attn_prompt(qa, kta, vtb)
            else:
                cache_kt, cache_vt, page_table, _ = cache
                q, k, v = _qkv_sample(h, w["g_mix"], w["w_sb_in"], i, j)
                new_k.append(k.reshape(b, t, d))
                new_v.append(v.reshape(b, t, d))
                pad = lambda a: _pad_rows(a.reshape(b, t, d), SUBLANES)
                o = _attn_sample(pad(q), pad(k), pad(v), cache_kt, cache_vt, j, page_table, w["b_sb"][j])[:, :t]
            h = _mm_res(o.reshape(n, d), w["w_sb_out"], j, h)
        else:
            q, k, v, og, la = _gla_in(h, w["g_mix"], w["w_gla_main"], w["w_gla_a"], w["w_gla_gate"],
                                      w["b_gla_gate"], i, j)
            shp = lambda a: a.reshape(b, t, a.shape[-1])
            if cache is None:
                c = GLA_CHUNK if t % GLA_CHUNK == 0 else t
                nsub = GLA_CHUNKS_PER_STEP if t % (c * GLA_CHUNKS_PER_STEP) == 0 else 1
                o, s = _gla(shp(q), shp(k), shp(la), shp(v), shp(og), w["g_gla_norm"], None, j, 1, c, nsub)
            else:
                c = 2 * SUBLANES
                pad = lambda a: _pad_rows(shp(a), c)
                o, s = _gla(pad(q), pad(k), pad(la), pad(v), pad(og), w["g_gla_norm"], cache[3], j, 8, c, 1)
                o = o[:, :t]
            new_s.append(s)
            h = _mm_res(o.reshape(n, d), w["w_gla_out"], j, h)
        h = _ffn(h, w["g_ffn"], w["w_ffn_in"], w["w_ffn_out"], i)
        h = _ple(h, w["g_ple"], w["w_ple_gate"], p, w["w_ple_proj"], w["g_final"], i, i == DEPTH - 1)
    return h.reshape(b, t, d), jnp.stack(new_k), jnp.stack(new_v), jnp.stack(new_s)


def _split_bias(bias):
    rest = bias * LOG2E
    parts = []
    for _ in range(BIAS_PARTS):
        part = rest.astype(BF).astype(F32)
        parts.append(part)
        rest = rest - part
    fill = jnp.stack(parts, axis=-1)
    fill = jnp.pad(fill, ((0, 0), (0, 0), (SB_HEAD_DIM, HEAD_PAD - SB_HEAD_DIM - BIAS_PARTS)))
    return fill.reshape(bias.shape[0], 1, SB_HEADS * HEAD_PAD)


def kernel(x_prompt, x_sample, cache_sb_k, cache_sb_v, state_gla, page_table, p_prompt, p_sample,
           g_mix, w_sb_in, w_sb_out, b_sb, w_gla_in, w_gla_gate, b_gla_gate, g_gla_norm, w_gla_out,
           g_ffn, w_ffn_in, w_ffn_out, g_ple, w_ple_gate, w_ple_proj, g_final):
    nmain = 2 * GLA_DK + 2 * GLA_DV
    n_sb = w_sb_in.shape[0]
    w_sb_bf = w_sb_in.astype(BF)
    wq_heads = w_sb_bf[:, :, :D_MODEL].reshape(n_sb, D_MODEL, SB_HEADS, SB_HEAD_DIM)
    rows = lambda a: a[:, None, :]
    w = dict(
        g_mix=rows(g_mix), b_sb=b_sb, b_gla_gate=rows(b_gla_gate), g_gla_norm=rows(g_gla_norm),
        g_ffn=rows(g_ffn), g_ple=rows(g_ple), g_final=g_final,
        w_sb_in=w_sb_bf, w_sb_out=w_sb_out.astype(BF),
        w_sb_q_pad=jnp.pad(wq_heads, ((0, 0), (0, 0), (0, 0), (0, HEAD_PAD - SB_HEAD_DIM))).reshape(
            n_sb, D_MODEL, SB_HEADS * HEAD_PAD),
        w_sb_kt=jnp.swapaxes(w_sb_bf[:, :, D_MODEL:2 * D_MODEL], 1, 2),
        w_sb_vt=jnp.swapaxes(w_sb_bf[:, :, 2 * D_MODEL:], 1, 2),
        sb_qfill=_split_bias(b_sb),
        w_gla_main=w_gla_in[:, :, :nmain].astype(BF),
        w_gla_a=jnp.pad(w_gla_in[:, :, nmain:], ((0, 0), (0, 0), (0, LANES - GLA_GATE_RANK))).astype(BF),
        w_gla_gate=jnp.pad(w_gla_gate, ((0, 0), (0, LANES - GLA_GATE_RANK), (0, 0))).astype(BF),
        w_gla_out=w_gla_out.astype(BF), w_ffn_in=w_ffn_in.astype(BF), w_ffn_out=w_ffn_out.astype(BF),
        w_ple_gate=w_ple_gate.astype(BF), w_ple_proj=w_ple_proj.astype(BF),
    )
    n_layers, n_pool = cache_sb_k.shape[0], cache_sb_k.shape[1]
    kt_view = lambda c: jnp.transpose(c, (0, 1, 3, 4, 2)).reshape(n_layers, n_pool, D_MODEL, PAGE_SIZE)
    cache = (kt_view(cache_sb_k), kt_view(cache_sb_v), page_table, state_gla)
    y_prompt, kt_prompt, vt_prompt, s_prompt = _trunk(x_prompt, p_prompt, w, None)
    y_sample, k_sample, v_sample, s_sample = _trunk(x_sample, p_sample, w, cache)
    bp, tp = x_prompt.shape[0], x_prompt.shape[1]
    bs, ts = x_sample.shape[0], x_sample.shape[1]
    heads_t = lambda a: jnp.transpose(a.reshape(n_sb, bp, SB_HEADS, SB_HEAD_DIM, tp), (0, 1, 4, 2, 3))
    heads = lambda a: a.reshape(n_sb, bs, ts, SB_HEADS, SB_HEAD_DIM)
    return (y_prompt, y_sample, heads_t(kt_prompt), heads_t(vt_prompt), s_prompt,
            heads(k_sample), heads(v_sample), s_sample)
```

```python
import functools

import jax
import jax.numpy as jnp
from jax import lax
from jax.experimental import pallas as pl
from jax.experimental.pallas import tpu as pltpu

BF = jnp.bfloat16
F32 = jnp.float32

D_MODEL = 1024
DEPTH = 4
PAGE_SIZE = 128
SB_HEADS = 16
SB_HEAD_DIM = D_MODEL // SB_HEADS
GLA_HEADS = 4
GLA_DK = D_MODEL // 2
GLA_DV = D_MODEL
GLA_HK = GLA_DK // GLA_HEADS
GLA_HV = GLA_DV // GLA_HEADS
GLA_GATE_RANK = 16
GLA_GATE_TAU = 16.0
GLA_CHUNK = 64
GLA_CHUNKS_PER_STEP = 4
D_FF = 2816
NORM_EPS = 1e-6

LOG2E = 1.4426950408889634
SB_QSCALE = SB_HEAD_DIM ** -0.5 * LOG2E
BIAS_PARTS = 3
LANES = 128
HEAD_PAD = LANES
SUBLANES = 8
VMEM_LIMIT = 56 * 1024 * 1024


def _params(*sem):
    return pltpu.CompilerParams(dimension_semantics=sem, vmem_limit_bytes=VMEM_LIMIT)


def _rms(x, g):
    return x * lax.rsqrt(jnp.mean(x * x, axis=-1, keepdims=True) + NORM_EPS) * g


def _dot(a, b):
    return jnp.dot(a, b, preferred_element_type=F32)


def _dot_nt(a, b):
    return lax.dot_general(a, b, (((1,), (1,)), ((), ())), preferred_element_type=F32)


def _dot_tn(a, b):
    return lax.dot_general(a, b, (((0,), (0,)), ((), ())), preferred_element_type=F32)


def _row_block(n, want):
    return want if n % want == 0 else n


def _tail_matrix(n):
    r = lax.broadcasted_iota(jnp.int32, (n, n), 0)
    c = lax.broadcasted_iota(jnp.int32, (n, n), 1)
    return jnp.where(r >= c, 1.0, 0.0).astype(BF)


def _sb_tile(z, mask, carry, tail_matrix):
    blk = tail_matrix.shape[0]
    p = jnp.maximum(z, 0.0) + jnp.log2(1.0 + jnp.exp2(-jnp.abs(z)))
    if mask is not None:
        p = jnp.where(mask, p, 0.0)
    nblk = z.shape[1] // blk
    tails = []
    for s in range(nblk):
        ps = p[:, s * blk:(s + 1) * blk]
        tails.append(_dot(ps.astype(BF), tail_matrix) + jnp.tile(carry, (1, blk // LANES)))
        if s < nblk - 1:
            carry = carry + jnp.sum(ps, axis=-1, keepdims=True)
    tail = tails[0] if nblk == 1 else jnp.concatenate(tails, axis=1)
    w = jnp.exp2(z - tail)
    if mask is not None:
        w = jnp.where(mask, w, 0.0)
    return w, carry + jnp.sum(p[:, (nblk - 1) * blk:], axis=-1, keepdims=True)


def _put_layer(ref, layer, first, val):
    if first:
        for l in range(ref.shape[0]):
            ref[l] = val if l == layer else jnp.zeros_like(val)
    else:
        ref[...] = val


def _qkv_prompt_body(x_ref, g_ref, wq_ref, wkt_ref, wvt_ref, qfill_ref, *refs, sb_layer, first):
    qa_ref, kt_ref, vt_ref, kta_ref, vtb_ref = refs[-5:]
    xn = _rms(x_ref[0], g_ref[...]).astype(BF)
    qa_ref[0] = (_dot(xn, wq_ref[...]) * SB_QSCALE + qfill_ref[...]).astype(BF)
    kt = _dot_nt(wkt_ref[...], xn)
    _put_layer(kt_ref, sb_layer, first, kt)
    tm = kt.shape[1]
    ones_rows = jnp.where(lax.broadcasted_iota(jnp.int32, (HEAD_PAD - SB_HEAD_DIM, tm), 0) < BIAS_PARTS, 1.0, 0.0)
    pieces = []
    for h in range(SB_HEADS):
        pieces += [kt[h * SB_HEAD_DIM:(h + 1) * SB_HEAD_DIM, :], ones_rows]
    kta_ref[0] = jnp.concatenate(pieces, axis=0).astype(BF)
    vt = _dot_nt(wvt_ref[...], xn)
    _put_layer(vt_ref, sb_layer, first, vt)
    vtb_ref[0] = vt.astype(BF)


def _qkv_prompt(x, g, wq_pad, wkt, wvt, qfill, layer, sb_layer, stacks):
    b, t, d = x.shape
    n_sb = wq_pad.shape[0]
    tm = _row_block(t, 512)
    dp = wq_pad.shape[2]
    first = stacks is None
    const = lambda shape, l: pl.BlockSpec((None,) + shape, lambda bi, i: (l, 0, 0))
    tr = lambda rows: pl.BlockSpec((1, rows, tm), lambda bi, i: (bi, 0, i))
    if first:
        stack_spec = pl.BlockSpec((n_sb, None, d, tm), lambda bi, i: (0, bi, 0, i))
    else:
        stack_spec = pl.BlockSpec((None, None, d, tm), lambda bi, i: (sb_layer, bi, 0, i))
    in_specs = [pl.BlockSpec((1, tm, d), lambda bi, i: (bi, i, 0)), const((1, d), layer),
                const((d, dp), sb_layer), const((d, d), sb_layer), const((d, d), sb_layer),
                const((1, dp), sb_layer)]
    args = [x, g, wq_pad, wkt, wvt, qfill]
    aliases = {}
    if not first:
        aliases = {len(args): 1, len(args) + 1: 2}
        in_specs += [pl.BlockSpec(memory_space=pl.ANY)] * 2
        args += list(stacks)
    stack_shape = jax.ShapeDtypeStruct((n_sb, b, d, t), F32)
    return pl.pallas_call(
        functools.partial(_qkv_prompt_body, sb_layer=sb_layer, first=first),
        grid=(b, t // tm),
        in_specs=in_specs,
        out_specs=[pl.BlockSpec((1, tm, dp), lambda bi, i: (bi, i, 0)), stack_spec, stack_spec, tr(dp), tr(d)],
        out_shape=[jax.ShapeDtypeStruct((b, t, dp), BF), stack_shape, stack_shape,
                   jax.ShapeDtypeStruct((b, dp, t), BF), jax.ShapeDtypeStruct((b, d, t), BF)],
        input_output_aliases=aliases,
        compiler_params=_params("parallel", "parallel"),
        name="sb_qkv_prompt",
    )(*args)


def _qkv_sample_body(h_ref, g_ref, w_ref, q_ref, k_ref, v_ref):
    d = h_ref.shape[1]
    xn = _rms(h_ref[...], g_ref[...]).astype(BF)
    q_ref[...] = _dot(xn, w_ref[:, 0:d]) * SB_QSCALE
    k_ref[...] = _dot(xn, w_ref[:, d:2 * d])
    v_ref[...] = _dot(xn, w_ref[:, 2 * d:3 * d])


def _qkv_sample(h, g, w, layer, sb_layer):
    n, d = h.shape
    tm = _row_block(n, 512)
    row = pl.BlockSpec((tm, d), lambda i: (i, 0))
    const = lambda shape, l: pl.BlockSpec((None,) + shape, lambda i: (l, 0, 0))
    return pl.pallas_call(
        _qkv_sample_body,
        grid=(n // tm,),
        in_specs=[row, const((1, d), layer), const((d, 3 * d), sb_layer)],
        out_specs=[row] * 3,
        out_shape=[jax.ShapeDtypeStruct((n, d), F32)] * 3,
        compiler_params=_params("parallel"),
        name="sb_qkv_sample",
    )(h, g, w)


def _attn_prompt_body(q_ref, k_ref, v_ref, o_ref, acc_ref, *, tq, tk):
    i = pl.program_id(2)
    tail_matrix = _tail_matrix(tk)
    acc_ref[...] = jnp.zeros_like(acc_ref)

    def block(kb, carries, masked, row0=0):
        rows = tq - row0
        ks = pl.multiple_of(kb * tk, tk)
        mask = None
        if masked:
            qpos = i * tq + row0 + lax.broadcasted_iota(jnp.int32, (rows, tk), 0)
            mask = ks + lax.broadcasted_iota(jnp.int32, (rows, tk), 1) < qpos
        vblk = v_ref[0, :, pl.ds(ks, tk)]
        out = []
        for h in range(2):
            hs = slice(h * HEAD_PAD, (h + 1) * HEAD_PAD)
            z = _dot(q_ref[0, row0:, hs], k_ref[0, hs, pl.ds(ks, tk)])
            w, c = _sb_tile(z, mask, carries[h][row0:], tail_matrix)
            acc_ref[h, row0:, :] += _dot_nt(w.astype(BF), vblk)
            out.append(c if row0 == 0 else jnp.concatenate([carries[h][:row0], c], axis=0))
        return tuple(out)

    nfull = (i * tq) // tk
    ndiag = max(tq // tk, 1)
    carries = (jnp.zeros((tq, LANES), F32), jnp.zeros((tq, LANES), F32))
    for c in reversed(range(ndiag)):
        carries = block(nfull + c, carries, True, row0=c * tk if tq > tk else 0)
    carries = lax.fori_loop(0, nfull // 2,
                            lambda t, c: block(nfull - 2 - 2 * t, block(nfull - 1 - 2 * t, c, False), False), carries)
    lax.fori_loop(0, nfull % 2, lambda t, c: block(0, c, False), carries)
    first = lax.broadcasted_iota(jnp.int32, (1, LANES), 1) < SB_HEAD_DIM
    o_ref[0] = jnp.where(first, acc_ref[0], acc_ref[1]).astype(o_ref.dtype)


def _attn_prompt(qa, kta, vtb, tq=1024, tk=256):
    b, t, dp = qa.shape
    d = vtb.shape[1]
    tq = min(tq, t)
    tk = min(tk, t)
    assert max(tq, tk) % min(tq, tk) == 0 and t % tq == 0 and t % tk == 0
    return pl.pallas_call(
        functools.partial(_attn_prompt_body, tq=tq, tk=tk),
        grid=(b, d // LANES, t // tq),
        in_specs=[pl.BlockSpec((1, tq, 2 * HEAD_PAD), lambda bi, hp, i: (bi, i, hp)),
                  pl.BlockSpec((1, 2 * HEAD_PAD, t), lambda bi, hp, i: (bi, hp, 0)),
                  pl.BlockSpec((1, LANES, t), lambda bi, hp, i: (bi, hp, 0))],
        out_specs=pl.BlockSpec((1, tq, LANES), lambda bi, hp, i: (bi, i, hp)),
        out_shape=jax.ShapeDtypeStruct((b, t, d), BF),
        scratch_shapes=[pltpu.VMEM((2, tq, LANES), F32)],
        compiler_params=_params("parallel", "parallel", "arbitrary"),
        name="sb_attn_prompt",
    )(qa, kta, vtb)


def _attn_sample_body(pt_ref, q_ref, kn_ref, vn_ref, bias_ref, *refs, pg):
    del pt_ref
    k_refs, v_refs = refs[:pg], refs[pg:2 * pg]
    o_ref, acc_ref, carry_ref = refs[2 * pg:]
    g = pl.program_id(1)
    tpad, d = q_ref.shape[1], q_ref.shape[2]
    rows = tpad * SB_HEADS

    head_of_lane = lax.broadcasted_iota(jnp.int32, (tpad, d), 1) // SB_HEAD_DIM
    q = q_ref[0]
    qrows = jnp.concatenate([jnp.where(head_of_lane == h, q, 0.0) for h in range(SB_HEADS)], axis=0).astype(BF)
    tail_matrix = _tail_matrix(PAGE_SIZE)
    bias = bias_ref[...] * LOG2E

    @pl.when(g == 0)
    def _():
        pad = jnp.zeros((PAGE_SIZE - tpad, d), F32)
        kn = jnp.concatenate([kn_ref[0], pad], axis=0).astype(BF)
        vn = jnp.concatenate([vn_ref[0], pad], axis=0).astype(BF)
        t_of_row = lax.broadcasted_iota(jnp.int32, (rows, PAGE_SIZE), 0) % tpad
        mask = lax.broadcasted_iota(jnp.int32, (rows, PAGE_SIZE), 1) < t_of_row
        w, c = _sb_tile(_dot_nt(qrows, kn) + bias[:, :PAGE_SIZE], mask, jnp.zeros((rows, LANES), F32), tail_matrix)
        acc_ref[...] = _dot(w.astype(BF), vn)
        carry_ref[...] = c

    kt = jnp.concatenate([r[...].astype(BF) for r in k_refs], axis=1)
    vt = jnp.concatenate([r[...].astype(BF) for r in v_refs], axis=1)
    w, carry = _sb_tile(_dot(qrows, kt) + bias, None, carry_ref[...], tail_matrix)
    acc_ref[...] += _dot_nt(w.astype(BF), vt)
    carry_ref[...] = carry

    @pl.when(g == pl.num_programs(1) - 1)
    def _():
        out = jnp.zeros((tpad, d), F32)
        for h in range(SB_HEADS):
            out = out + jnp.where(head_of_lane == h, acc_ref[h * tpad:(h + 1) * tpad, :], 0.0)
        o_ref[0] = out


def _attn_sample(q, k_new, v_new, cache_kt, cache_vt, layer, page_table, bias, pg=8):
    b, tpad, d = q.shape
    n_pages = page_table.shape[1]
    pg = min(pg, n_pages)
    rows = tpad * SB_HEADS
    bias_tile = jnp.broadcast_to(jnp.repeat(bias, tpad)[:, None], (rows, pg * PAGE_SIZE))
    tok = pl.BlockSpec((1, tpad, d), lambda bi, g, pt: (bi, 0, 0))

    def page_spec(s):
        return pl.BlockSpec((None, None, d, PAGE_SIZE),
                            lambda bi, g, pt: (layer, pt[bi, n_pages - 1 - (g * pg + s)], 0, 0))

    specs = [page_spec(s) for s in range(pg)]
    grid_spec = pltpu.PrefetchScalarGridSpec(
        num_scalar_prefetch=1,
        grid=(b, n_pages // pg),
        in_specs=[tok, tok, tok, pl.BlockSpec((rows, pg * PAGE_SIZE), lambda bi, g, pt: (0, 0))] + specs + specs,
        out_specs=tok,
        scratch_shapes=[pltpu.VMEM((rows, d), F32), pltpu.VMEM((rows, LANES), F32)],
    )
    return pl.pallas_call(
        functools.partial(_attn_sample_body, pg=pg),
        grid_spec=grid_spec,
        out_shape=jax.ShapeDtypeStruct((b, tpad, d), F32),
        compiler_params=_params("parallel", "arbitrary"),
        name="sb_attn_sample",
    )(page_table, q, k_new, v_new, bias_tile, *([cache_kt] * pg), *([cache_vt] * pg))


def _mm_res_body(x_ref, w_ref, r_ref, o_ref):
    o_ref[...] = r_ref[...] + _dot(x_ref[...].astype(BF), w_ref[...])


def _mm_res(x, w, layer, res):
    n, kdim = x.shape
    d = w.shape[2]
    tm = _row_block(n, 512)
    return pl.pallas_call(
        _mm_res_body,
        grid=(n // tm,),
        in_specs=[pl.BlockSpec((tm, kdim), lambda i: (i, 0)),
                  pl.BlockSpec((None, kdim, d), lambda i: (layer, 0, 0)),
                  pl.BlockSpec((tm, d), lambda i: (i, 0))],
        out_specs=pl.BlockSpec((tm, d), lambda i: (i, 0)),
        out_shape=jax.ShapeDtypeStruct((n, d), F32),
        compiler_params=_params("parallel"),
        name="mm_res",
    )(x, w, res)


def _gla_in_body(h_ref, g_ref, w_ref, wa_ref, wg_ref, bg_ref, q_ref, k_ref, v_ref, og_ref, la_ref):
    xn = _rms(h_ref[...], g_ref[...]).astype(BF)
    q_ref[...] = _dot(xn, w_ref[:, 0:GLA_DK]) * (GLA_HK ** -0.5)
    k_ref[...] = _dot(xn, w_ref[:, GLA_DK:2 * GLA_DK])
    v_ref[...] = _dot(xn, w_ref[:, 2 * GLA_DK:2 * GLA_DK + GLA_DV])
    og_ref[...] = _dot(xn, w_ref[:, 2 * GLA_DK + GLA_DV:2 * GLA_DK + 2 * GLA_DV])
    a = _dot(xn, wa_ref[...])
    x = _dot(a.astype(BF), wg_ref[...]) + bg_ref[...]
    la_ref[...] = (jnp.minimum(x, 0.0) - jnp.log(1.0 + jnp.exp(-jnp.abs(x)))) / GLA_GATE_TAU


def _gla_in(h, g, w_main, w_a, w_gate, b_gate, layer, gla_layer):
    n, d = h.shape
    tm = _row_block(n, 512)
    nmain = w_main.shape[2]
    const = lambda shape, l: pl.BlockSpec((None,) + shape, lambda i: (l, 0, 0))
    row = lambda width: pl.BlockSpec((tm, width), lambda i: (i, 0))
    return pl.pallas_call(
        _gla_in_body,
        grid=(n // tm,),
        in_specs=[row(d), const((1, d), layer), const((d, nmain), gla_layer), const((d, LANES), gla_layer),
                  const((LANES, GLA_DK), gla_layer), const((1, GLA_DK), gla_layer)],
        out_specs=[row(GLA_DK), row(GLA_DK), row(GLA_DV), row(GLA_DV), row(GLA_DK)],
        out_shape=[jax.ShapeDtypeStruct((n, GLA_DK), F32), jax.ShapeDtypeStruct((n, GLA_DK), F32),
                   jax.ShapeDtypeStruct((n, GLA_DV), F32), jax.ShapeDtypeStruct((n, GLA_DV), F32),
                   jax.ShapeDtypeStruct((n, GLA_DK), F32)],
        compiler_params=_params("parallel"),
        name="gla_in",
    )(h, g, w_main, w_a, w_gate, b_gate)


def _split3(x):
    hi = x.astype(BF)
    r = x - hi.astype(F32)
    mid = r.astype(BF)
    lo = (r - mid.astype(F32)).astype(BF)
    return hi, mid, lo


def _gla_body(*refs, has_state, c, layer, stack_first):
    q_ref, k_ref, la_ref, v_ref, og_ref, gn_ref = refs[:6]
    s0_ref = refs[6] if has_state else None
    o_ref, s_out = refs[-2:]
    bb, nsub = q_ref.shape[0], q_ref.shape[1] // c
    s_ref = s_out.at[layer] if stack_first else s_out

    @pl.when(pl.program_id(1) == 0)
    def _():
        if stack_first:
            for l in range(s_out.shape[0]):
                if l != layer:
                    s_out[l] = jnp.zeros(s_out.shape[1:], s_out.dtype)
        s_ref[...] = s0_ref[...] if has_state else jnp.zeros_like(s_ref)

    r = lax.broadcasted_iota(jnp.int32, (c, c), 0)
    col = lax.broadcasted_iota(jnp.int32, (c, c), 1)
    causal = col <= r
    lower = jnp.where(causal, 1.0, 0.0).astype(BF)
    ones = jnp.ones((c, GLA_HK), BF)
    gn = gn_ref[...]

    def one(i, _):
        state = [s_ref[i, h] for h in range(GLA_HEADS)]
        for j in range(nsub):
            rows = slice(j * c, (j + 1) * c)
            parts = _split3(la_ref[i, rows, :])
            b = _dot(lower, parts[0]) + _dot(lower, parts[1]) + _dot(lower, parts[2])
            b_col = _dot_tn(parts[0], ones) + _dot_tn(parts[1], ones) + _dot_tn(parts[2], ones)
            b_last = b[c - 1:c, :]
            k = k_ref[i, rows, :]
            qg = (q_ref[i, rows, :] * jnp.exp(b)).astype(BF)
            kg = (k * jnp.exp(-b)).astype(BF)
            kd = (k * jnp.exp(b_last - b)).astype(BF)
            decay = jnp.exp(b_col)
            for h in range(GLA_HEADS):
                ks = slice(h * GLA_HK, (h + 1) * GLA_HK)
                vs = slice(h * GLA_HV, (h + 1) * GLA_HV)
                v = v_ref[i, rows, vs].astype(BF)
                a = jnp.where(causal, _dot_nt(qg[:, ks], kg[:, ks]), 0.0)
                o = _dot(a.astype(BF), v) + _dot(qg[:, ks], state[h].astype(BF))
                state[h] = (jnp.concatenate([decay[ks, :]] * (GLA_HV // GLA_HK), axis=1) * state[h]
                            + _dot_tn(kd[:, ks], v))
                og = og_ref[i, rows, vs]
                o_ref[i, rows, vs] = _rms(o, gn) * (og * jax.nn.sigmoid(og))
        for h in range(GLA_HEADS):
            s_ref[i, h] = state[h]
        return 0

    if bb == 1:
        one(0, 0)
    else:
        lax.fori_loop(0, bb, one, 0)


def _gla(q, k, la, v, og, gnorm, s0, layer, bb, c, nsub, stack=None):
    b, t, _ = q.shape
    n_layers = gnorm.shape[0]
    has_state = s0 is not None
    state = (GLA_HEADS, GLA_HK, GLA_HV)
    tok = lambda width: pl.BlockSpec((bb, c * nsub, width), lambda bi, ci: (bi, ci, 0))
    in_specs = [tok(GLA_DK), tok(GLA_DK), tok(GLA_DK), tok(GLA_DV), tok(GLA_DV),
                pl.BlockSpec((None, 1, GLA_HV), lambda bi, ci: (layer, 0, 0))]
    args = [q, k, la, v, og, gnorm]
    if has_state:
        in_specs.append(pl.BlockSpec((None, bb) + state, lambda bi, ci: (layer, bi, 0, 0, 0)))
        args.append(s0)
    aliases = {}
    if stack is None:
        sspec = pl.BlockSpec((bb,) + state, lambda bi, ci: (bi, 0, 0, 0))
        sshape = jax.ShapeDtypeStruct((b,) + state, F32)
    else:
        sshape = jax.ShapeDtypeStruct((n_layers, b) + state, F32)
        if isinstance(stack, str):
            sspec = pl.BlockSpec((n_layers, bb) + state, lambda bi, ci: (0, bi, 0, 0, 0))
        else:
            sspec = pl.BlockSpec((None, bb) + state, lambda bi, ci: (layer, bi, 0, 0, 0))
            aliases = {len(args): 1}
            in_specs.append(pl.BlockSpec(memory_space=pl.ANY))
            args.append(stack)
    return pl.pallas_call(
        functools.partial(_gla_body, has_state=has_state, c=c, layer=layer, stack_first=isinstance(stack, str)),
        grid=(b // bb, t // (c * nsub)),
        in_specs=in_specs,
        out_specs=[tok(GLA_DV), sspec],
        out_shape=[jax.ShapeDtypeStruct((b, t, GLA_DV), F32), sshape],
        input_output_aliases=aliases,
        compiler_params=_params("parallel", "arbitrary"),
        name="gla",
    )(*args)


def _ffn_body(h_ref, g_ref, wg_ref, wu_ref, wo_ref, o_ref, xn_ref, acc_ref):
    f = pl.program_id(1)

    @pl.when(f == 0)
    def _():
        xn_ref[...] = _rms(h_ref[...], g_ref[...]).astype(BF)
        acc_ref[...] = jnp.zeros_like(acc_ref)

    xn = xn_ref[...]
    gate = _dot(xn, wg_ref[...])
    up = _dot(xn, wu_ref[...])
    acc_ref[...] += _dot((gate * jax.nn.sigmoid(gate) * up).astype(BF), wo_ref[...])

    @pl.when(f == pl.num_programs(1) - 1)
    def _():
        o_ref[...] = h_ref[...] + acc_ref[...]


def _ffn(h, g, w_in, w_out, layer, tf=1408):
    n, d = h.shape
    dff = w_out.shape[1]
    tm = _row_block(n, 512)
    nf = dff // tf
    row = pl.BlockSpec((tm, d), lambda i, f: (i, 0))
    return pl.pallas_call(
        _ffn_body,
        grid=(n // tm, nf),
        in_specs=[row, pl.BlockSpec((None, 1, d), lambda i, f: (layer, 0, 0)),
                  pl.BlockSpec((None, d, tf), lambda i, f: (layer, 0, f)),
                  pl.BlockSpec((None, d, tf), lambda i, f: (layer, 0, f + nf)),
                  pl.BlockSpec((None, tf, d), lambda i, f: (layer, f, 0))],
        out_specs=row,
        out_shape=jax.ShapeDtypeStruct((n, d), F32),
        scratch_shapes=[pltpu.VMEM((tm, d), BF), pltpu.VMEM((tm, d), F32)],
        compiler_params=_params("parallel", "arbitrary"),
        name="ffn",
    )(h, g, w_in, w_in, w_out)


def _ple_body(h_ref, g_ref, wg_ref, p_ref, wp_ref, gf_ref, o_ref, *, final):
    h = h_ref[...]
    gate = jax.nn.sigmoid(_dot(_rms(h, g_ref[...]).astype(BF), wg_ref[...]))
    out = h + gate * _dot(p_ref[...].astype(BF), wp_ref[...])
    o_ref[...] = _rms(out, gf_ref[...]) if final else out


def _ple(h, g, w_gate, p, w_proj, g_final, layer, final):
    n, d = h.shape
    pd = p.shape[2]
    tm = _row_block(n, 512)
    const = lambda shape: pl.BlockSpec((None,) + shape, lambda i: (layer, 0, 0))
    row = lambda width: pl.BlockSpec((tm, width), lambda i: (i, 0))
    return pl.pallas_call(
        functools.partial(_ple_body, final=final),
        grid=(n // tm,),
        in_specs=[row(d), const((1, d)), const((d, d)), pl.BlockSpec((None, tm, pd), lambda i: (layer, i, 0)),
                  const((pd, d)), pl.BlockSpec((1, d), lambda i: (0, 0))],
        out_specs=row(d),
        out_shape=jax.ShapeDtypeStruct((n, d), F32),
        compiler_params=_params("parallel"),
        name="ple",
    )(h, g, w_gate, p, w_proj, g_final.reshape(1, d))


def _pad_rows(x, rows):
    return jnp.pad(x, ((0, 0), (0, rows - x.shape[1]), (0, 0)))


def _trunk(x, p, w, cache):
    b, t, d = x.shape
    n = b * t
    h = x.reshape(n, d)
    p = p.reshape(DEPTH, n, -1)
    new_k, new_v, new_s = [], [], []
    kv_stacks, s_stack = None, "first"
    for i in range(DEPTH):
        j = i // 2
        if i % 2 == 0:
            if cache is None:
                qa, kt, vt, kta, vtb = _qkv_prompt(h.reshape(b, t, d), w["g_mix"], w["w_sb_q_pad"], w["w_sb_kt"],
                                                   w["w_sb_vt"], w["sb_qfill"], i, j, kv_stacks)
                kv_stacks = (kt, vt)
                o = _attn_prompt(qa, kta, vtb)
            else:
                cache_kt, cache_vt, page_table, _ = cache
                q, k, v = _qkv_sample(h, w["g_mix"], w["w_sb_in"], i, j)
                new_k.append(k.reshape(b, t, d))
                new_v.append(v.reshape(b, t, d))
                pad = lambda a: _pad_rows(a.reshape(b, t, d), SUBLANES)
                o = _attn_sample(pad(q), pad(k), pad(v), cache_kt, cache_vt, j, page_table, w["b_sb"][j])[:, :t]
            h = _mm_res(o.reshape(n, d), w["w_sb_out"], j, h)
        else:
            q, k, v, og, la = _gla_in(h, w["g_mix"], w["w_gla_main"], w["w_gla_a"], w["w_gla_gate"],
                                      w["b_gla_gate"], i, j)
            shp = lambda a: a.reshape(b, t, a.shape[-1])
            if cache is None:
                c = GLA_CHUNK if t % GLA_CHUNK == 0 else t
                nsub = GLA_CHUNKS_PER_STEP if t % (c * GLA_CHUNKS_PER_STEP) == 0 else 1
                o, s = _gla(shp(q), shp(k), shp(la), shp(v), shp(og), w["g_gla_norm"], None, j, 1, c, nsub)
                new_s.append(s)
            else:
                c = 2 * SUBLANES
                pad = lambda a: _pad_rows(shp(a), c)
                o, s_stack = _gla(pad(q), pad(k), pad(la), pad(v), pad(og), w["g_gla_norm"], cache[3], j, 8, c, 1,
                                  stack=s_stack)
                o = o[:, :t]
            h = _mm_res(o.reshape(n, d), w["w_gla_out"], j, h)
        h = _ffn(h, w["g_ffn"], w["w_ffn_in"], w["w_ffn_out"], i)
        h = _ple(h, w["g_ple"], w["w_ple_gate"], p, w["w_ple_proj"], w["g_final"], i, i == DEPTH - 1)
    if cache is None:
        return h.reshape(b, t, d), kv_stacks[0], kv_stacks[1], jnp.stack(new_s)
    return h.reshape(b, t, d), jnp.stack(new_k), jnp.stack(new_v), s_stack


def _split_bias(bias):
    rest = bias * LOG2E
    parts = []
    for _ in range(BIAS_PARTS):
        part = rest.astype(BF).astype(F32)
        parts.append(part)
        rest = rest - part
    fill = jnp.stack(parts, axis=-1)
    fill = jnp.pad(fill, ((0, 0), (0, 0), (SB_HEAD_DIM, HEAD_PAD - SB_HEAD_DIM - BIAS_PARTS)))
    return fill.reshape(bias.shape[0], 1, SB_HEADS * HEAD_PAD)


def kernel(x_prompt, x_sample, cache_sb_k, cache_sb_v, state_gla, page_table, p_prompt, p_sample,
           g_mix, w_sb_in, w_sb_out, b_sb, w_gla_in, w_gla_gate, b_gla_gate, g_gla_norm, w_gla_out,
           g_ffn, w_ffn_in, w_ffn_out, g_ple, w_ple_gate, w_ple_proj, g_final):
    nmain = 2 * GLA_DK + 2 * GLA_DV
    n_sb = w_sb_in.shape[0]
    w_sb_bf = w_sb_in.astype(BF)
    wq_heads = w_sb_bf[:, :, :D_MODEL].reshape(n_sb, D_MODEL, SB_HEADS, SB_HEAD_DIM)
    rows = lambda a: a[:, None, :]
    w = dict(
        g_mix=rows(g_mix), b_sb=b_sb, b_gla_gate=rows(b_gla_gate), g_gla_norm=rows(g_gla_norm),
        g_ffn=rows(g_ffn), g_ple=rows(g_ple), g_final=g_final,
        w_sb_in=w_sb_bf, w_sb_out=w_sb_out.astype(BF),
        w_sb_q_pad=jnp.pad(wq_heads, ((0, 0), (0, 0), (0, 0), (0, HEAD_PAD - SB_HEAD_DIM))).reshape(
            n_sb, D_MODEL, SB_HEADS * HEAD_PAD),
        w_sb_kt=jnp.swapaxes(w_sb_bf[:, :, D_MODEL:2 * D_MODEL], 1, 2),
        w_sb_vt=jnp.swapaxes(w_sb_bf[:, :, 2 * D_MODEL:], 1, 2),
        sb_qfill=_split_bias(b_sb),
        w_gla_main=w_gla_in[:, :, :nmain].astype(BF),
        w_gla_a=jnp.pad(w_gla_in[:, :, nmain:], ((0, 0), (0, 0), (0, LANES - GLA_GATE_RANK))).astype(BF),
        w_gla_gate=jnp.pad(w_gla_gate, ((0, 0), (0, LANES - GLA_GATE_RANK), (0, 0))).astype(BF),
        w_gla_out=w_gla_out.astype(BF), w_ffn_in=w_ffn_in.astype(BF), w_ffn_out=w_ffn_out.astype(BF),
        w_ple_gate=w_ple_gate.astype(BF), w_ple_proj=w_ple_proj.astype(BF),
    )
    n_layers, n_pool = cache_sb_k.shape[0], cache_sb_k.shape[1]
    kt_view = lambda c: jnp.transpose(c, (0, 1, 3, 4, 2)).reshape(n_layers, n_pool, D_MODEL, PAGE_SIZE)
    cache = (kt_view(cache_sb_k), kt_view(cache_sb_v), page_table, state_gla)
    y_prompt, kt_prompt, vt_prompt, s_prompt = _trunk(x_prompt, p_prompt, w, None)
    y_sample, k_sample, v_sample, s_sample = _trunk(x_sample, p_sample, w, cache)
    bp, tp = x_prompt.shape[0], x_prompt.shape[1]
    bs, ts = x_sample.shape[0], x_sample.shape[1]
    heads_t = lambda a: jnp.transpose(a.reshape(n_sb, bp, SB_HEADS, SB_HEAD_DIM, tp), (0, 1, 4, 2, 3))
    heads = lambda a: a.reshape(n_sb, bs, ts, SB_HEADS, SB_HEAD_DIM)
    return (y_prompt, y_sample, heads_t(kt_prompt), heads_t(vt_prompt), s_prompt,
            heads(k_sample), heads(v_sample), s_sample)
```

```python
import functools

import jax
import jax.numpy as jnp
from jax import lax
from jax.experimental import pallas as pl
from jax.experimental.pallas import tpu as pltpu

BF = jnp.bfloat16
F32 = jnp.float32

D_MODEL = 1024
DEPTH = 4
PAGE_SIZE = 128
SB_HEADS = 16
SB_HEAD_DIM = D_MODEL // SB_HEADS
GLA_HEADS = 4
GLA_DK = D_MODEL // 2
GLA_DV = D_MODEL
GLA_HK = GLA_DK // GLA_HEADS
GLA_HV = GLA_DV // GLA_HEADS
GLA_GATE_RANK = 16
GLA_GATE_TAU = 16.0
GLA_CHUNK = 64
GLA_CHUNKS_PER_STEP = 4
D_FF = 2816
NORM_EPS = 1e-6

LOG2E = 1.4426950408889634
SB_QSCALE = SB_HEAD_DIM ** -0.5 * LOG2E
BIAS_PARTS = 3
LANES = 128
HEAD_PAD = LANES
SUBLANES = 8
VMEM_LIMIT = 56 * 1024 * 1024


def _params(*sem):
    return pltpu.CompilerParams(dimension_semantics=sem, vmem_limit_bytes=VMEM_LIMIT)


def _rms(x, g):
    return x * lax.rsqrt(jnp.mean(x * x, axis=-1, keepdims=True) + NORM_EPS) * g


def _dot(a, b):
    return jnp.dot(a, b, preferred_element_type=F32)


def _dot_nt(a, b):
    return lax.dot_general(a, b, (((1,), (1,)), ((), ())), preferred_element_type=F32)


def _dot_tn(a, b):
    return lax.dot_general(a, b, (((0,), (0,)), ((), ())), preferred_element_type=F32)


def _row_block(n, want):
    return want if n % want == 0 else n


def _tail_matrix(n):
    r = lax.broadcasted_iota(jnp.int32, (n, n), 0)
    c = lax.broadcasted_iota(jnp.int32, (n, n), 1)
    return jnp.where(r >= c, 1.0, 0.0).astype(BF)


def _sb_tile(z, mask, carry, tail_matrix):
    blk = tail_matrix.shape[0]
    p = jnp.maximum(z, 0.0) + jnp.log2(1.0 + jnp.exp2(-jnp.abs(z)))
    if mask is not None:
        p = jnp.where(mask, p, 0.0)
    nblk = z.shape[1] // blk
    tails = []
    for s in range(nblk):
        ps = p[:, s * blk:(s + 1) * blk]
        tails.append(_dot(ps.astype(BF), tail_matrix) + jnp.tile(carry, (1, blk // LANES)))
        if s < nblk - 1:
            carry = carry + jnp.sum(ps, axis=-1, keepdims=True)
    tail = tails[0] if nblk == 1 else jnp.concatenate(tails, axis=1)
    w = jnp.exp2(z - tail)
    if mask is not None:
        w = jnp.where(mask, w, 0.0)
    return w, carry + jnp.sum(p[:, (nblk - 1) * blk:], axis=-1, keepdims=True)


def _put_layer(ref, layer, first, val):
    if first:
        for l in range(ref.shape[0]):
            ref[l] = val if l == layer else jnp.zeros_like(val)
    else:
        ref[...] = val


def _qkv_prompt_body(x_ref, g_ref, wq_ref, wkt_ref, wvt_ref, qfill_ref, *refs, sb_layer, first):
    qa_ref, kt_ref, vt_ref, kta_ref, vtb_ref = refs[-5:]
    xn = _rms(x_ref[0], g_ref[...]).astype(BF)
    qa_ref[0] = (_dot(xn, wq_ref[...]) * SB_QSCALE + qfill_ref[...]).astype(BF)
    kt = _dot_nt(wkt_ref[...], xn)
    _put_layer(kt_ref, sb_layer, first, kt)
    tm = kt.shape[1]
    ones_rows = jnp.where(lax.broadcasted_iota(jnp.int32, (HEAD_PAD - SB_HEAD_DIM, tm), 0) < BIAS_PARTS, 1.0, 0.0)
    pieces = []
    for h in range(SB_HEADS):
        pieces += [kt[h * SB_HEAD_DIM:(h + 1) * SB_HEAD_DIM, :], ones_rows]
    kta_ref[0] = jnp.concatenate(pieces, axis=0).astype(BF)
    vt = _dot_nt(wvt_ref[...], xn)
    _put_layer(vt_ref, sb_layer, first, vt)
    vtb_ref[0] = vt.astype(BF)


def _qkv_prompt(x, g, wq_pad, wkt, wvt, qfill, layer, sb_layer, stacks):
    b, t, d = x.shape
    n_sb = wq_pad.shape[0]
    tm = _row_block(t, 512)
    dp = wq_pad.shape[2]
    first = stacks is None
    const = lambda shape, l: pl.BlockSpec((None,) + shape, lambda bi, i: (l, 0, 0))
    tr = lambda rows: pl.BlockSpec((1, rows, tm), lambda bi, i: (bi, 0, i))
    if first:
        stack_spec = pl.BlockSpec((n_sb, None, d, tm), lambda bi, i: (0, bi, 0, i))
    else:
        stack_spec = pl.BlockSpec((None, None, d, tm), lambda bi, i: (sb_layer, bi, 0, i))
    in_specs = [pl.BlockSpec((1, tm, d), lambda bi, i: (bi, i, 0)), const((1, d), layer),
                const((d, dp), sb_layer), const((d, d), sb_layer), const((d, d), sb_layer),
                const((1, dp), sb_layer)]
    args = [x, g, wq_pad, wkt, wvt, qfill]
    aliases = {}
    if not first:
        aliases = {len(args): 1, len(args) + 1: 2}
        in_specs += [pl.BlockSpec(memory_space=pl.ANY)] * 2
        args += list(stacks)
    stack_shape = jax.ShapeDtypeStruct((n_sb, b, d, t), F32)
    return pl.pallas_call(
        functools.partial(_qkv_prompt_body, sb_layer=sb_layer, first=first),
        grid=(b, t // tm),
        in_specs=in_specs,
        out_specs=[pl.BlockSpec((1, tm, dp), lambda bi, i: (bi, i, 0)), stack_spec, stack_spec, tr(dp), tr(d)],
        out_shape=[jax.ShapeDtypeStruct((b, t, dp), BF), stack_shape, stack_shape,
                   jax.ShapeDtypeStruct((b, dp, t), BF), jax.ShapeDtypeStruct((b, d, t), BF)],
        input_output_aliases=aliases,
        compiler_params=_params("parallel", "parallel"),
        name="sb_qkv_prompt",
    )(*args)


def _qkv_sample_body(h_ref, g_ref, w_ref, q_ref, k_ref, v_ref):
    d = h_ref.shape[1]
    xn = _rms(h_ref[...], g_ref[...]).astype(BF)
    q_ref[...] = _dot(xn, w_ref[:, 0:d]) * SB_QSCALE
    k_ref[...] = _dot(xn, w_ref[:, d:2 * d])
    v_ref[...] = _dot(xn, w_ref[:, 2 * d:3 * d])


def _qkv_sample(h, g, w, layer, sb_layer):
    n, d = h.shape
    tm = _row_block(n, 512)
    row = pl.BlockSpec((tm, d), lambda i: (i, 0))
    const = lambda shape, l: pl.BlockSpec((None,) + shape, lambda i: (l, 0, 0))
    return pl.pallas_call(
        _qkv_sample_body,
        grid=(n // tm,),
        in_specs=[row, const((1, d), layer), const((d, 3 * d), sb_layer)],
        out_specs=[row] * 3,
        out_shape=[jax.ShapeDtypeStruct((n, d), F32)] * 3,
        compiler_params=_params("parallel"),
        name="sb_qkv_sample",
    )(h, g, w)


def _attn_prompt_body(q_ref, k_ref, v_ref, o_ref, acc_ref, *, tq, tk):
    i = pl.program_id(2)
    tail_matrix = _tail_matrix(tk)
    acc_ref[...] = jnp.zeros_like(acc_ref)

    def block(kb, carries, masked, row0=0):
        rows = tq - row0
        ks = pl.multiple_of(kb * tk, tk)
        mask = None
        if masked:
            qpos = i * tq + row0 + lax.broadcasted_iota(jnp.int32, (rows, tk), 0)
            mask = ks + lax.broadcasted_iota(jnp.int32, (rows, tk), 1) < qpos
        vblk = v_ref[0, :, pl.ds(ks, tk)]
        out = []
        for h in range(2):
            hs = slice(h * HEAD_PAD, (h + 1) * HEAD_PAD)
            z = _dot(q_ref[0, row0:, hs], k_ref[0, hs, pl.ds(ks, tk)])
            w, c = _sb_tile(z, mask, carries[h][row0:], tail_matrix)
            acc_ref[h, row0:, :] += _dot_nt(w.astype(BF), vblk)
            out.append(c if row0 == 0 else jnp.concatenate([carries[h][:row0], c], axis=0))
        return tuple(out)

    nfull = (i * tq) // tk
    ndiag = max(tq // tk, 1)
    carries = (jnp.zeros((tq, LANES), F32), jnp.zeros((tq, LANES), F32))
    for c in reversed(range(ndiag)):
        carries = block(nfull + c, carries, True, row0=c * tk if tq > tk else 0)
    carries = lax.fori_loop(0, nfull // 2,
                            lambda t, c: block(nfull - 2 - 2 * t, block(nfull - 1 - 2 * t, c, False), False), carries)
    lax.fori_loop(0, nfull % 2, lambda t, c: block(0, c, False), carries)
    first = lax.broadcasted_iota(jnp.int32, (1, LANES), 1) < SB_HEAD_DIM
    o_ref[0] = jnp.where(first, acc_ref[0], acc_ref[1]).astype(o_ref.dtype)


def _attn_prompt(qa, kta, vtb, tq=1024, tk=256):
    b, t, dp = qa.shape
    d = vtb.shape[1]
    tq = min(tq, t)
    tk = min(tk, t)
    assert max(tq, tk) % min(tq, tk) == 0 and t % tq == 0 and t % tk == 0
    return pl.pallas_call(
        functools.partial(_attn_prompt_body, tq=tq, tk=tk),
        grid=(b, d // LANES, t // tq),
        in_specs=[pl.BlockSpec((1, tq, 2 * HEAD_PAD), lambda bi, hp, i: (bi, i, hp)),
                  pl.BlockSpec((1, 2 * HEAD_PAD, t), lambda bi, hp, i: (bi, hp, 0)),
                  pl.BlockSpec((1, LANES, t), lambda bi, hp, i: (bi, hp, 0))],
        out_specs=pl.BlockSpec((1, tq, LANES), lambda bi, hp, i: (bi, i, hp)),
        out_shape=jax.ShapeDtypeStruct((b, t, d), BF),
        scratch_shapes=[pltpu.VMEM((2, tq, LANES), F32)],
        compiler_params=_params("parallel", "parallel", "arbitrary"),
        name="sb_attn_prompt",
    )(qa, kta, vtb)


def _attn_sample_body(pt_ref, q_ref, kn_ref, vn_ref, bias_ref, *refs, pg):
    del pt_ref
    k_refs, v_refs = refs[:pg], refs[pg:2 * pg]
    o_ref, acc_ref, carry_ref = refs[2 * pg:]
    g = pl.program_id(1)
    tpad, d = q_ref.shape[1], q_ref.shape[2]
    rows = tpad * SB_HEADS

    head_of_lane = lax.broadcasted_iota(jnp.int32, (tpad, d), 1) // SB_HEAD_DIM
    q = q_ref[0]
    qrows = jnp.concatenate([jnp.where(head_of_lane == h, q, 0.0) for h in range(SB_HEADS)], axis=0).astype(BF)
    tail_matrix = _tail_matrix(PAGE_SIZE)
    bias = bias_ref[...] * LOG2E

    @pl.when(g == 0)
    def _():
        pad = jnp.zeros((PAGE_SIZE - tpad, d), F32)
        kn = jnp.concatenate([kn_ref[0], pad], axis=0).astype(BF)
        vn = jnp.concatenate([vn_ref[0], pad], axis=0).astype(BF)
        t_of_row = lax.broadcasted_iota(jnp.int32, (rows, PAGE_SIZE), 0) % tpad
        mask = lax.broadcasted_iota(jnp.int32, (rows, PAGE_SIZE), 1) < t_of_row
        w, c = _sb_tile(_dot_nt(qrows, kn) + bias[:, :PAGE_SIZE], mask, jnp.zeros((rows, LANES), F32), tail_matrix)
        acc_ref[...] = _dot(w.astype(BF), vn)
        carry_ref[...] = c

    kt = jnp.concatenate([r[...].astype(BF) for r in k_refs], axis=1)
    vt = jnp.concatenate([r[...].astype(BF) for r in v_refs], axis=1)
    w, carry = _sb_tile(_dot(qrows, kt) + bias, None, carry_ref[...], tail_matrix)
    acc_ref[...] += _dot_nt(w.astype(BF), vt)
    carry_ref[...] = carry

    @pl.when(g == pl.num_programs(1) - 1)
    def _():
        out = jnp.zeros((tpad, d), F32)
        for h in range(SB_HEADS):
            out = out + jnp.where(head_of_lane == h, acc_ref[h * tpad:(h + 1) * tpad, :], 0.0)
        o_ref[0] = out


def _attn_sample(q, k_new, v_new, cache_kt, cache_vt, layer, page_table, bias, pg=16):
    b, tpad, d = q.shape
    n_pages = page_table.shape[1]
    pg = min(pg, n_pages)
    rows = tpad * SB_HEADS
    bias_tile = jnp.broadcast_to(jnp.repeat(bias, tpad)[:, None], (rows, pg * PAGE_SIZE))
    tok = pl.BlockSpec((1, tpad, d), lambda bi, g, pt: (bi, 0, 0))

    def page_spec(s):
        return pl.BlockSpec((None, None, d, PAGE_SIZE),
                            lambda bi, g, pt: (layer, pt[bi, n_pages - 1 - (g * pg + s)], 0, 0))

    specs = [page_spec(s) for s in range(pg)]
    grid_spec = pltpu.PrefetchScalarGridSpec(
        num_scalar_prefetch=1,
        grid=(b, n_pages // pg),
        in_specs=[tok, tok, tok, pl.BlockSpec((rows, pg * PAGE_SIZE), lambda bi, g, pt: (0, 0))] + specs + specs,
        out_specs=tok,
        scratch_shapes=[pltpu.VMEM((rows, d), F32), pltpu.VMEM((rows, LANES), F32)],
    )
    return pl.pallas_call(
        functools.partial(_attn_sample_body, pg=pg),
        grid_spec=grid_spec,
        out_shape=jax.ShapeDtypeStruct((b, tpad, d), F32),
        compiler_params=_params("parallel", "arbitrary"),
        name="sb_attn_sample",
    )(page_table, q, k_new, v_new, bias_tile, *([cache_kt] * pg), *([cache_vt] * pg))


def _mm_res_body(x_ref, w_ref, r_ref, o_ref):
    o_ref[...] = r_ref[...] + _dot(x_ref[...].astype(BF), w_ref[...])


def _mm_res(x, w, layer, res):
    n, kdim = x.shape
    d = w.shape[2]
    tm = _row_block(n, 512)
    return pl.pallas_call(
        _mm_res_body,
        grid=(n // tm,),
        in_specs=[pl.BlockSpec((tm, kdim), lambda i: (i, 0)),
                  pl.BlockSpec((None, kdim, d), lambda i: (layer, 0, 0)),
                  pl.BlockSpec((tm, d), lambda i: (i, 0))],
        out_specs=pl.BlockSpec((tm, d), lambda i: (i, 0)),
        out_shape=jax.ShapeDtypeStruct((n, d), F32),
        compiler_params=_params("parallel"),
        name="mm_res",
    )(x, w, res)


def _gla_in_body(h_ref, g_ref, w_ref, wa_ref, wg_ref, bg_ref, q_ref, k_ref, v_ref, og_ref, la_ref):
    xn = _rms(h_ref[...], g_ref[...]).astype(BF)
    q_ref[...] = _dot(xn, w_ref[:, 0:GLA_DK]) * (GLA_HK ** -0.5)
    k_ref[...] = _dot(xn, w_ref[:, GLA_DK:2 * GLA_DK])
    v_ref[...] = _dot(xn, w_ref[:, 2 * GLA_DK:2 * GLA_DK + GLA_DV])
    og_ref[...] = _dot(xn, w_ref[:, 2 * GLA_DK + GLA_DV:2 * GLA_DK + 2 * GLA_DV])
    a = _dot(xn, wa_ref[...])
    x = _dot(a.astype(BF), wg_ref[...]) + bg_ref[...]
    la_ref[...] = (jnp.minimum(x, 0.0) - jnp.log(1.0 + jnp.exp(-jnp.abs(x)))) / GLA_GATE_TAU


def _gla_in(h, g, w_main, w_a, w_gate, b_gate, layer, gla_layer):
    n, d = h.shape
    tm = _row_block(n, 512)
    nmain = w_main.shape[2]
    const = lambda shape, l: pl.BlockSpec((None,) + shape, lambda i: (l, 0, 0))
    row = lambda width: pl.BlockSpec((tm, width), lambda i: (i, 0))
    return pl.pallas_call(
        _gla_in_body,
        grid=(n // tm,),
        in_specs=[row(d), const((1, d), layer), const((d, nmain), gla_layer), const((d, LANES), gla_layer),
                  const((LANES, GLA_DK), gla_layer), const((1, GLA_DK), gla_layer)],
        out_specs=[row(GLA_DK), row(GLA_DK), row(GLA_DV), row(GLA_DV), row(GLA_DK)],
        out_shape=[jax.ShapeDtypeStruct((n, GLA_DK), F32), jax.ShapeDtypeStruct((n, GLA_DK), F32),
                   jax.ShapeDtypeStruct((n, GLA_DV), F32), jax.ShapeDtypeStruct((n, GLA_DV), F32),
                   jax.ShapeDtypeStruct((n, GLA_DK), F32)],
        compiler_params=_params("parallel"),
        name="gla_in",
    )(h, g, w_main, w_a, w_gate, b_gate)


def _split3(x):
    hi = x.astype(BF)
    r = x - hi.astype(F32)
    mid = r.astype(BF)
    lo = (r - mid.astype(F32)).astype(BF)
    return hi, mid, lo


def _gla_body(*refs, has_state, c, layer, stack_first):
    q_ref, k_ref, la_ref, v_ref, og_ref, gn_ref = refs[:6]
    s0_ref = refs[6] if has_state else None
    o_ref, s_out = refs[-2:]
    bb, nsub = q_ref.shape[0], q_ref.shape[1] // c
    s_ref = s_out.at[layer] if stack_first else s_out

    @pl.when(pl.program_id(1) == 0)
    def _():
        if stack_first:
            for l in range(s_out.shape[0]):
                if l != layer:
                    s_out[l] = jnp.zeros(s_out.shape[1:], s_out.dtype)
        s_ref[...] = s0_ref[...] if has_state else jnp.zeros_like(s_ref)

    r = lax.broadcasted_iota(jnp.int32, (c, c), 0)
    col = lax.broadcasted_iota(jnp.int32, (c, c), 1)
    causal = col <= r
    lower = jnp.where(causal, 1.0, 0.0).astype(BF)
    ones = jnp.ones((c, GLA_HK), BF)
    gn = gn_ref[...]

    def one(i, _):
        state = [s_ref[i, h] for h in range(GLA_HEADS)]
        for j in range(nsub):
            rows = slice(j * c, (j + 1) * c)
            parts = _split3(la_ref[i, rows, :])
            b = _dot(lower, parts[0]) + _dot(lower, parts[1]) + _dot(lower, parts[2])
            b_col = _dot_tn(parts[0], ones) + _dot_tn(parts[1], ones) + _dot_tn(parts[2], ones)
            b_last = b[c - 1:c, :]
            k = k_ref[i, rows, :]
            qg = (q_ref[i, rows, :] * jnp.exp(b)).astype(BF)
            kg = (k * jnp.exp(-b)).astype(BF)
            kd = (k * jnp.exp(b_last - b)).astype(BF)
            decay = jnp.exp(b_col)
            for h in range(GLA_HEADS):
                ks = slice(h * GLA_HK, (h + 1) * GLA_HK)
                vs = slice(h * GLA_HV, (h + 1) * GLA_HV)
                v = v_ref[i, rows, vs].astype(BF)
                a = jnp.where(causal, _dot_nt(qg[:, ks], kg[:, ks]), 0.0)
                o = _dot(a.astype(BF), v) + _dot(qg[:, ks], state[h].astype(BF))
                state[h] = (jnp.concatenate([decay[ks, :]] * (GLA_HV // GLA_HK), axis=1) * state[h]
                            + _dot_tn(kd[:, ks], v))
                og = og_ref[i, rows, vs]
                o_ref[i, rows, vs] = _rms(o, gn) * (og * jax.nn.sigmoid(og))
        for h in range(GLA_HEADS):
            s_ref[i, h] = state[h]
        return 0

    if bb == 1:
        one(0, 0)
    else:
        lax.fori_loop(0, bb // 2, lambda ii, _: one(2 * ii + 1, one(2 * ii, 0)), 0)


def _gla(q, k, la, v, og, gnorm, s0, layer, bb, c, nsub, stack=None):
    b, t, _ = q.shape
    n_layers = gnorm.shape[0]
    has_state = s0 is not None
    state = (GLA_HEADS, GLA_HK, GLA_HV)
    tok = lambda width: pl.BlockSpec((bb, c * nsub, width), lambda bi, ci: (bi, ci, 0))
    in_specs = [tok(GLA_DK), tok(GLA_DK), tok(GLA_DK), tok(GLA_DV), tok(GLA_DV),
                pl.BlockSpec((None, 1, GLA_HV), lambda bi, ci: (layer, 0, 0))]
    args = [q, k, la, v, og, gnorm]
    if has_state:
        in_specs.append(pl.BlockSpec((None, bb) + state, lambda bi, ci: (layer, bi, 0, 0, 0)))
        args.append(s0)
    aliases = {}
    if stack is None:
        sspec = pl.BlockSpec((bb,) + state, lambda bi, ci: (bi, 0, 0, 0))
        sshape = jax.ShapeDtypeStruct((b,) + state, F32)
    else:
        sshape = jax.ShapeDtypeStruct((n_layers, b) + state, F32)
        if isinstance(stack, str):
            sspec = pl.BlockSpec((n_layers, bb) + state, lambda bi, ci: (0, bi, 0, 0, 0))
        else:
            sspec = pl.BlockSpec((None, bb) + state, lambda bi, ci: (layer, bi, 0, 0, 0))
            aliases = {len(args): 1}
            in_specs.append(pl.BlockSpec(memory_space=pl.ANY))
            args.append(stack)
    return pl.pallas_call(
        functools.partial(_gla_body, has_state=has_state, c=c, layer=layer, stack_first=isinstance(stack, str)),
        grid=(b // bb, t // (c * nsub)),
        in_specs=in_specs,
        out_specs=[tok(GLA_DV), sspec],
        out_shape=[jax.ShapeDtypeStruct((b, t, GLA_DV), F32), sshape],
        input_output_aliases=aliases,
        compiler_params=_params("parallel", "arbitrary"),
        name="gla",
    )(*args)


def _ffn_body(h_ref, g_ref, wg_ref, wu_ref, wo_ref, gp_ref, wpg_ref, p_ref, wpp_ref, gf_ref, o_ref, xn_ref, acc_ref,
              *, final):
    f = pl.program_id(1)

    @pl.when(f == 0)
    def _():
        xn_ref[...] = _rms(h_ref[...], g_ref[...]).astype(BF)
        acc_ref[...] = jnp.zeros_like(acc_ref)

    xn = xn_ref[...]
    gate = _dot(xn, wg_ref[...])
    up = _dot(xn, wu_ref[...])
    acc_ref[...] += _dot((gate * jax.nn.sigmoid(gate) * up).astype(BF), wo_ref[...])

    @pl.when(f == pl.num_programs(1) - 1)
    def _():
        h = h_ref[...] + acc_ref[...]
        gate = jax.nn.sigmoid(_dot(_rms(h, gp_ref[...]).astype(BF), wpg_ref[...]))
        out = h + gate * _dot(p_ref[...].astype(BF), wpp_ref[...])
        o_ref[...] = _rms(out, gf_ref[...]) if final else out


def _ffn_ple(h, g, w_in, w_out, g_ple, w_gate, p, w_proj, g_final, layer, final, tf=1408):
    n, d = h.shape
    dff = w_out.shape[1]
    pd = p.shape[2]
    tm = _row_block(n, 512)
    nf = dff // tf
    row = pl.BlockSpec((tm, d), lambda i, f: (i, 0))
    const = lambda shape: pl.BlockSpec((None,) + shape, lambda i, f: (layer, 0, 0))
    return pl.pallas_call(
        functools.partial(_ffn_body, final=final),
        grid=(n // tm, nf),
        in_specs=[row, const((1, d)),
                  pl.BlockSpec((None, d, tf), lambda i, f: (layer, 0, f)),
                  pl.BlockSpec((None, d, tf), lambda i, f: (layer, 0, f + nf)),
                  pl.BlockSpec((None, tf, d), lambda i, f: (layer, f, 0)),
                  const((1, d)), const((d, d)), pl.BlockSpec((None, tm, pd), lambda i, f: (layer, i, 0)),
                  const((pd, d)), pl.BlockSpec((1, d), lambda i, f: (0, 0))],
        out_specs=row,
        out_shape=jax.ShapeDtypeStruct((n, d), F32),
        scratch_shapes=[pltpu.VMEM((tm, d), BF), pltpu.VMEM((tm, d), F32)],
        compiler_params=_params("parallel", "arbitrary"),
        name="ffn_ple",
    )(h, g, w_in, w_in, w_out, g_ple, w_gate, p, w_proj, g_final.reshape(1, d))


def _pad_rows(x, rows):
    return jnp.pad(x, ((0, 0), (0, rows - x.shape[1]), (0, 0)))


def _trunk(x, p, w, cache):
    b, t, d = x.shape
    n = b * t
    h = x.reshape(n, d)
    p = p.reshape(DEPTH, n, -1)
    new_k, new_v, new_s = [], [], []
    kv_stacks, s_stack = None, "first"
    for i in range(DEPTH):
        j = i // 2
        if i % 2 == 0:
            if cache is None:
                qa, kt, vt, kta, vtb = _qkv_prompt(h.reshape(b, t, d), w["g_mix"], w["w_sb_q_pad"], w["w_sb_kt"],
                                                   w["w_sb_vt"], w["sb_qfill"], i, j, kv_stacks)
                kv_stacks = (kt, vt)
                o = _attn_prompt(qa, kta, vtb)
            else:
                cache_kt, cache_vt, page_table, _ = cache
                q, k, v = _qkv_sample(h, w["g_mix"], w["w_sb_in"], i, j)
                new_k.append(k.reshape(b, t, d))
                new_v.append(v.reshape(b, t, d))
                pad = lambda a: _pad_rows(a.reshape(b, t, d), SUBLANES)
                o = _attn_sample(pad(q), pad(k), pad(v), cache_kt, cache_vt, j, page_table, w["b_sb"][j])[:, :t]
            h = _mm_res(o.reshape(n, d), w["w_sb_out"], j, h)
        else:
            q, k, v, og, la = _gla_in(h, w["g_mix"], w["w_gla_main"], w["w_gla_a"], w["w_gla_gate"],
                                      w["b_gla_gate"], i, j)
            shp = lambda a: a.reshape(b, t, a.shape[-1])
            if cache is None:
                c = GLA_CHUNK if t % GLA_CHUNK == 0 else t
                nsub = GLA_CHUNKS_PER_STEP if t % (c * GLA_CHUNKS_PER_STEP) == 0 else 1
                o, s = _gla(shp(q), shp(k), shp(la), shp(v), shp(og), w["g_gla_norm"], None, j, 1, c, nsub)
                new_s.append(s)
            else:
                c = 2 * SUBLANES
                pad = lambda a: _pad_rows(shp(a), c)
                o, s_stack = _gla(pad(q), pad(k), pad(la), pad(v), pad(og), w["g_gla_norm"], cache[3], j, 8, c, 1,
                                  stack=s_stack)
                o = o[:, :t]
            h = _mm_res(o.reshape(n, d), w["w_gla_out"], j, h)
        h = _ffn_ple(h, w["g_ffn"], w["w_ffn_in"], w["w_ffn_out"], w["g_ple"], w["w_ple_gate"], p, w["w_ple_proj"],
                     w["g_final"], i, i == DEPTH - 1)
    if cache is None:
        return h.reshape(b, t, d), kv_stacks[0], kv_stacks[1], jnp.stack(new_s)
    return h.reshape(b, t, d), jnp.stack(new_k), jnp.stack(new_v), s_stack


def _split_bias(bias):
    rest = bias * LOG2E
    parts = []
    for _ in range(BIAS_PARTS):
        part = rest.astype(BF).astype(F32)
        parts.append(part)
        rest = rest - part
    fill = jnp.stack(parts, axis=-1)
    fill = jnp.pad(fill, ((0, 0), (0, 0), (SB_HEAD_DIM, HEAD_PAD - SB_HEAD_DIM - BIAS_PARTS)))
    return fill.reshape(bias.shape[0], 1, SB_HEADS * HEAD_PAD)


def kernel(x_prompt, x_sample, cache_sb_k, cache_sb_v, state_gla, page_table, p_prompt, p_sample,
           g_mix, w_sb_in, w_sb_out, b_sb, w_gla_in, w_gla_gate, b_gla_gate, g_gla_norm, w_gla_out,
           g_ffn, w_ffn_in, w_ffn_out, g_ple, w_ple_gate, w_ple_proj, g_final):
    nmain = 2 * GLA_DK + 2 * GLA_DV
    n_sb = w_sb_in.shape[0]
    w_sb_bf = w_sb_in.astype(BF)
    wq_heads = w_sb_bf[:, :, :D_MODEL].reshape(n_sb, D_MODEL, SB_HEADS, SB_HEAD_DIM)
    rows = lambda a: a[:, None, :]
    w = dict(
        g_mix=rows(g_mix), b_sb=b_sb, b_gla_gate=rows(b_gla_gate), g_gla_norm=rows(g_gla_norm),
        g_ffn=rows(g_ffn), g_ple=rows(g_ple), g_final=g_final,
        w_sb_in=w_sb_bf, w_sb_out=w_sb_out.astype(BF),
        w_sb_q_pad=jnp.pad(wq_heads, ((0, 0), (0, 0), (0, 0), (0, HEAD_PAD - SB_HEAD_DIM))).reshape(
            n_sb, D_MODEL, SB_HEADS * HEAD_PAD),
        w_sb_kt=jnp.swapaxes(w_sb_bf[:, :, D_MODEL:2 * D_MODEL], 1, 2),
        w_sb_vt=jnp.swapaxes(w_sb_bf[:, :, 2 * D_MODEL:], 1, 2),
        sb_qfill=_split_bias(b_sb),
        w_gla_main=w_gla_in[:, :, :nmain].astype(BF),
        w_gla_a=jnp.pad(w_gla_in[:, :, nmain:], ((0, 0), (0, 0), (0, LANES - GLA_GATE_RANK))).astype(BF),
        w_gla_gate=jnp.pad(w_gla_gate, ((0, 0), (0, LANES - GLA_GATE_RANK), (0, 0))).astype(BF),
        w_gla_out=w_gla_out.astype(BF), w_ffn_in=w_ffn_in.astype(BF), w_ffn_out=w_ffn_out.astype(BF),
        w_ple_gate=w_ple_gate.astype(BF), w_ple_proj=w_ple_proj.astype(BF),
    )
    n_layers, n_pool = cache_sb_k.shape[0], cache_sb_k.shape[1]
    kt_view = lambda c: jnp.transpose(c, (0, 1, 3, 4, 2)).reshape(n_layers, n_pool, D_MODEL, PAGE_SIZE)
    cache = (kt_view(cache_sb_k), kt_view(cache_sb_v), page_table, state_gla)
    y_prompt, kt_prompt, vt_prompt, s_prompt = _trunk(x_prompt, p_prompt, w, None)
    y_sample, k_sample, v_sample, s_sample = _trunk(x_sample, p_sample, w, cache)
    bp, tp = x_prompt.shape[0], x_prompt.shape[1]
    bs, ts = x_sample.shape[0], x_sample.shape[1]
    heads_t = lambda a: jnp.transpose(a.reshape(n_sb, bp, SB_HEADS, SB_HEAD_DIM, tp), (0, 1, 4, 2, 3))
    heads = lambda a: a.reshape(n_sb, bs, ts, SB_HEADS, SB_HEAD_DIM)
    return (y_prompt, y_sample, heads_t(kt_prompt), heads_t(vt_prompt), s_prompt,
            heads(k_sample), heads(v_sample), s_sample)
```

```python
import functools

import jax
import jax.numpy as jnp
from jax import lax
from jax.experimental import pallas as pl
from jax.experimental.pallas import tpu as pltpu

BF = jnp.bfloat16
F32 = jnp.float32

D_MODEL = 1024
DEPTH = 4
PAGE_SIZE = 128
SB_HEADS = 16
SB_HEAD_DIM = D_MODEL // SB_HEADS
GLA_HEADS = 4
GLA_DK = D_MODEL // 2
GLA_DV = D_MODEL
GLA_HK = GLA_DK // GLA_HEADS
GLA_HV = GLA_DV // GLA_HEADS
GLA_GATE_RANK = 16
GLA_GATE_TAU = 16.0
GLA_CHUNK = 64
GLA_CHUNKS_PER_STEP = 8
D_FF = 2816
NORM_EPS = 1e-6

LOG2E = 1.4426950408889634
SB_QSCALE = SB_HEAD_DIM ** -0.5 * LOG2E
BIAS_PARTS = 3
LANES = 128
HEAD_PAD = LANES
SUBLANES = 8
VMEM_LIMIT = 56 * 1024 * 1024


def _params(*sem):
    return pltpu.CompilerParams(dimension_semantics=sem, vmem_limit_bytes=VMEM_LIMIT)


def _rms(x, g):
    return x * lax.rsqrt(jnp.mean(x * x, axis=-1, keepdims=True) + NORM_EPS) * g


def _dot(a, b):
    return jnp.dot(a, b, preferred_element_type=F32)


def _dot_nt(a, b):
    return lax.dot_general(a, b, (((1,), (1,)), ((), ())), preferred_element_type=F32)


def _dot_tn(a, b):
    return lax.dot_general(a, b, (((0,), (0,)), ((), ())), preferred_element_type=F32)


def _row_block(n, want):
    return want if n % want == 0 else n


def _tail_matrix(n):
    r = lax.broadcasted_iota(jnp.int32, (n, n), 0)
    c = lax.broadcasted_iota(jnp.int32, (n, n), 1)
    return jnp.where(r >= c, 1.0, 0.0).astype(BF)


def _sb_tile(z, mask, carry, tail_matrix):
    blk = tail_matrix.shape[0]
    p = jnp.maximum(z, 0.0) + jnp.log2(1.0 + jnp.exp2(-jnp.abs(z)))
    if mask is not None:
        p = jnp.where(mask, p, 0.0)
    nblk = z.shape[1] // blk
    tails = []
    for s in range(nblk):
        ps = p[:, s * blk:(s + 1) * blk]
        tails.append(_dot(ps.astype(BF), tail_matrix) + jnp.tile(carry, (1, blk // LANES)))
        if s < nblk - 1:
            carry = carry + jnp.sum(ps, axis=-1, keepdims=True)
    tail = tails[0] if nblk == 1 else jnp.concatenate(tails, axis=1)
    w = jnp.exp2(z - tail)
    if mask is not None:
        w = jnp.where(mask, w, 0.0)
    return w, carry + jnp.sum(p[:, (nblk - 1) * blk:], axis=-1, keepdims=True)


def _put_layer(ref, layer, first, val):
    if first:
        for l in range(ref.shape[0]):
            ref[l] = val if l == layer else jnp.zeros_like(val)
    else:
        ref[...] = val


def _qkv_prompt_body(x_ref, g_ref, wq_ref, wkt_ref, wvt_ref, qfill_ref, *refs, sb_layer, first):
    qa_ref, kt_ref, vt_ref, kta_ref, vtb_ref = refs[-5:]
    xn = _rms(x_ref[0], g_ref[...]).astype(BF)
    qa_ref[0] = (_dot(xn, wq_ref[...]) * SB_QSCALE + qfill_ref[...]).astype(BF)
    kt = _dot_nt(wkt_ref[...], xn)
    _put_layer(kt_ref, sb_layer, first, kt)
    tm = kt.shape[1]
    ones_rows = jnp.where(lax.broadcasted_iota(jnp.int32, (HEAD_PAD - SB_HEAD_DIM, tm), 0) < BIAS_PARTS, 1.0, 0.0)
    pieces = []
    for h in range(SB_HEADS):
        pieces += [kt[h * SB_HEAD_DIM:(h + 1) * SB_HEAD_DIM, :], ones_rows]
    kta_ref[0] = jnp.concatenate(pieces, axis=0).astype(BF)
    vt = _dot_nt(wvt_ref[...], xn)
    _put_layer(vt_ref, sb_layer, first, vt)
    vtb_ref[0] = vt.astype(BF)


def _qkv_prompt(x, g, wq_pad, wkt, wvt, qfill, layer, sb_layer, stacks):
    b, t, d = x.shape
    n_sb = wq_pad.shape[0]
    tm = _row_block(t, 512)
    dp = wq_pad.shape[2]
    first = stacks is None
    const = lambda shape, l: pl.BlockSpec((None,) + shape, lambda bi, i: (l, 0, 0))
    tr = lambda rows: pl.BlockSpec((1, rows, tm), lambda bi, i: (bi, 0, i))
    if first:
        stack_spec = pl.BlockSpec((n_sb, None, d, tm), lambda bi, i: (0, bi, 0, i))
    else:
        stack_spec = pl.BlockSpec((None, None, d, tm), lambda bi, i: (sb_layer, bi, 0, i))
    in_specs = [pl.BlockSpec((1, tm, d), lambda bi, i: (bi, i, 0)), const((1, d), layer),
                const((d, dp), sb_layer), const((d, d), sb_layer), const((d, d), sb_layer),
                const((1, dp), sb_layer)]
    args = [x, g, wq_pad, wkt, wvt, qfill]
    aliases = {}
    if not first:
        aliases = {len(args): 1, len(args) + 1: 2}
        in_specs += [pl.BlockSpec(memory_space=pl.ANY)] * 2
        args += list(stacks)
    stack_shape = jax.ShapeDtypeStruct((n_sb, b, d, t), F32)
    return pl.pallas_call(
        functools.partial(_qkv_prompt_body, sb_layer=sb_layer, first=first),
        grid=(b, t // tm),
        in_specs=in_specs,
        out_specs=[pl.BlockSpec((1, tm, dp), lambda bi, i: (bi, i, 0)), stack_spec, stack_spec, tr(dp), tr(d)],
        out_shape=[jax.ShapeDtypeStruct((b, t, dp), BF), stack_shape, stack_shape,
                   jax.ShapeDtypeStruct((b, dp, t), BF), jax.ShapeDtypeStruct((b, d, t), BF)],
        input_output_aliases=aliases,
        compiler_params=_params("parallel", "parallel"),
        name="sb_qkv_prompt",
    )(*args)


def _qkv_sample_body(h_ref, g_ref, w_ref, q_ref, k_ref, v_ref):
    d = h_ref.shape[1]
    xn = _rms(h_ref[...], g_ref[...]).astype(BF)
    q_ref[...] = _dot(xn, w_ref[:, 0:d]) * SB_QSCALE
    k_ref[...] = _dot(xn, w_ref[:, d:2 * d])
    v_ref[...] = _dot(xn, w_ref[:, 2 * d:3 * d])


def _qkv_sample(h, g, w, layer, sb_layer):
    n, d = h.shape
    tm = _row_block(n, 512)
    row = pl.BlockSpec((tm, d), lambda i: (i, 0))
    const = lambda shape, l: pl.BlockSpec((None,) + shape, lambda i: (l, 0, 0))
    return pl.pallas_call(
        _qkv_sample_body,
        grid=(n // tm,),
        in_specs=[row, const((1, d), layer), const((d, 3 * d), sb_layer)],
        out_specs=[row] * 3,
        out_shape=[jax.ShapeDtypeStruct((n, d), F32)] * 3,
        compiler_params=_params("parallel"),
        name="sb_qkv_sample",
    )(h, g, w)


def _attn_prompt_body(q_ref, k_ref, v_ref, o_ref, acc_ref, *, tq, tk):
    i = pl.program_id(2)
    tail_matrix = _tail_matrix(tk)
    acc_ref[...] = jnp.zeros_like(acc_ref)

    def block(kb, carries, masked, row0=0):
        rows = tq - row0
        ks = pl.multiple_of(kb * tk, tk)
        mask = None
        if masked:
            qpos = i * tq + row0 + lax.broadcasted_iota(jnp.int32, (rows, tk), 0)
            mask = ks + lax.broadcasted_iota(jnp.int32, (rows, tk), 1) < qpos
        vblk = v_ref[0, :, pl.ds(ks, tk)]
        out = []
        for h in range(2):
            hs = slice(h * HEAD_PAD, (h + 1) * HEAD_PAD)
            z = _dot(q_ref[0, row0:, hs], k_ref[0, hs, pl.ds(ks, tk)])
            w, c = _sb_tile(z, mask, carries[h][row0:], tail_matrix)
            acc_ref[h, row0:, :] += _dot_nt(w.astype(BF), vblk)
            out.append(c if row0 == 0 else jnp.concatenate([carries[h][:row0], c], axis=0))
        return tuple(out)

    nfull = (i * tq) // tk
    ndiag = max(tq // tk, 1)
    carries = (jnp.zeros((tq, LANES), F32), jnp.zeros((tq, LANES), F32))
    for c in reversed(range(ndiag)):
        carries = block(nfull + c, carries, True, row0=c * tk if tq > tk else 0)
    carries = lax.fori_loop(0, nfull // 2,
                            lambda t, c: block(nfull - 2 - 2 * t, block(nfull - 1 - 2 * t, c, False), False), carries)
    lax.fori_loop(0, nfull % 2, lambda t, c: block(0, c, False), carries)
    first = lax.broadcasted_iota(jnp.int32, (1, LANES), 1) < SB_HEAD_DIM
    o_ref[0] = jnp.where(first, acc_ref[0], acc_ref[1]).astype(o_ref.dtype)


def _attn_prompt(qa, kta, vtb, tq=2048, tk=256):
    b, t, dp = qa.shape
    d = vtb.shape[1]
    tq = min(tq, t)
    tk = min(tk, t)
    assert max(tq, tk) % min(tq, tk) == 0 and t % tq == 0 and t % tk == 0
    return pl.pallas_call(
        functools.partial(_attn_prompt_body, tq=tq, tk=tk),
        grid=(b, d // LANES, t // tq),
        in_specs=[pl.BlockSpec((1, tq, 2 * HEAD_PAD), lambda bi, hp, i: (bi, i, hp)),
                  pl.BlockSpec((1, 2 * HEAD_PAD, t), lambda bi, hp, i: (bi, hp, 0)),
                  pl.BlockSpec((1, LANES, t), lambda bi, hp, i: (bi, hp, 0))],
        out_specs=pl.BlockSpec((1, tq, LANES), lambda bi, hp, i: (bi, i, hp)),
        out_shape=jax.ShapeDtypeStruct((b, t, d), BF),
        scratch_shapes=[pltpu.VMEM((2, tq, LANES), F32)],
        compiler_params=_params("parallel", "parallel", "arbitrary"),
        name="sb_attn_prompt",
    )(qa, kta, vtb)


def _attn_sample_body(pt_ref, q_ref, kn_ref, vn_ref, bias_ref, *refs, pg):
    del pt_ref
    k_refs, v_refs = refs[:pg], refs[pg:2 * pg]
    o_ref, acc_ref, carry_ref = refs[2 * pg:]
    g = pl.program_id(1)
    tpad, d = q_ref.shape[1], q_ref.shape[2]
    rows = tpad * SB_HEADS

    head_of_lane = lax.broadcasted_iota(jnp.int32, (tpad, d), 1) // SB_HEAD_DIM
    q = q_ref[0]
    qrows = jnp.concatenate([jnp.where(head_of_lane == h, q, 0.0) for h in range(SB_HEADS)], axis=0).astype(BF)
    tail_matrix = _tail_matrix(PAGE_SIZE)
    bias = bias_ref[...] * LOG2E

    @pl.when(g == 0)
    def _():
        pad = jnp.zeros((PAGE_SIZE - tpad, d), F32)
        kn = jnp.concatenate([kn_ref[0], pad], axis=0).astype(BF)
        vn = jnp.concatenate([vn_ref[0], pad], axis=0).astype(BF)
        t_of_row = lax.broadcasted_iota(jnp.int32, (rows, PAGE_SIZE), 0) % tpad
        mask = lax.broadcasted_iota(jnp.int32, (rows, PAGE_SIZE), 1) < t_of_row
        w, c = _sb_tile(_dot_nt(qrows, kn) + bias[:, :PAGE_SIZE], mask, jnp.zeros((rows, LANES), F32), tail_matrix)
        acc_ref[...] = _dot(w.astype(BF), vn)
        carry_ref[...] = c

    kt = jnp.concatenate([r[...].astype(BF) for r in k_refs], axis=1)
    vt = jnp.concatenate([r[...].astype(BF) for r in v_refs], axis=1)
    w, carry = _sb_tile(_dot(qrows, kt) + bias, None, carry_ref[...], tail_matrix)
    acc_ref[...] += _dot_nt(w.astype(BF), vt)
    carry_ref[...] = carry

    @pl.when(g == pl.num_programs(1) - 1)
    def _():
        out = jnp.zeros((tpad, d), F32)
        for h in range(SB_HEADS):
            out = out + jnp.where(head_of_lane == h, acc_ref[h * tpad:(h + 1) * tpad, :], 0.0)
        o_ref[0] = out


def _attn_sample(q, k_new, v_new, cache_kt, cache_vt, layer, page_table, bias, pg=16):
    b, tpad, d = q.shape
    n_pages = page_table.shape[1]
    pg = min(pg, n_pages)
    rows = tpad * SB_HEADS
    bias_tile = jnp.broadcast_to(jnp.repeat(bias, tpad)[:, None], (rows, pg * PAGE_SIZE))
    tok = pl.BlockSpec((1, tpad, d), lambda bi, g, pt: (bi, 0, 0))

    def page_spec(s):
        return pl.BlockSpec((None, None, d, PAGE_SIZE),
                            lambda bi, g, pt: (layer, pt[bi, n_pages - 1 - (g * pg + s)], 0, 0))

    specs = [page_spec(s) for s in range(pg)]
    grid_spec = pltpu.PrefetchScalarGridSpec(
        num_scalar_prefetch=1,
        grid=(b, n_pages // pg),
        in_specs=[tok, tok, tok, pl.BlockSpec((rows, pg * PAGE_SIZE), lambda bi, g, pt: (0, 0))] + specs + specs,
        out_specs=tok,
        scratch_shapes=[pltpu.VMEM((rows, d), F32), pltpu.VMEM((rows, LANES), F32)],
    )
    return pl.pallas_call(
        functools.partial(_attn_sample_body, pg=pg),
        grid_spec=grid_spec,
        out_shape=jax.ShapeDtypeStruct((b, tpad, d), F32),
        compiler_params=_params("parallel", "arbitrary"),
        name="sb_attn_sample",
    )(page_table, q, k_new, v_new, bias_tile, *([cache_kt] * pg), *([cache_vt] * pg))


def _mm_res_body(x_ref, w_ref, r_ref, o_ref):
    o_ref[...] = r_ref[...] + _dot(x_ref[...].astype(BF), w_ref[...])


def _mm_res(x, w, layer, res):
    n, kdim = x.shape
    d = w.shape[2]
    tm = _row_block(n, 512)
    return pl.pallas_call(
        _mm_res_body,
        grid=(n // tm,),
        in_specs=[pl.BlockSpec((tm, kdim), lambda i: (i, 0)),
                  pl.BlockSpec((None, kdim, d), lambda i: (layer, 0, 0)),
                  pl.BlockSpec((tm, d), lambda i: (i, 0))],
        out_specs=pl.BlockSpec((tm, d), lambda i: (i, 0)),
        out_shape=jax.ShapeDtypeStruct((n, d), F32),
        compiler_params=_params("parallel"),
        name="mm_res",
    )(x, w, res)


def _gla_in_body(h_ref, g_ref, w_ref, wa_ref, wg_ref, bg_ref, q_ref, k_ref, v_ref, og_ref, la_ref):
    xn = _rms(h_ref[...], g_ref[...]).astype(BF)
    q_ref[...] = _dot(xn, w_ref[:, 0:GLA_DK]) * (GLA_HK ** -0.5)
    k_ref[...] = _dot(xn, w_ref[:, GLA_DK:2 * GLA_DK])
    v_ref[...] = _dot(xn, w_ref[:, 2 * GLA_DK:2 * GLA_DK + GLA_DV])
    og_ref[...] = _dot(xn, w_ref[:, 2 * GLA_DK + GLA_DV:2 * GLA_DK + 2 * GLA_DV])
    a = _dot(xn, wa_ref[...])
    x = _dot(a.astype(BF), wg_ref[...]) + bg_ref[...]
    la_ref[...] = (jnp.minimum(x, 0.0) - jnp.log(1.0 + jnp.exp(-jnp.abs(x)))) / GLA_GATE_TAU


def _gla_in(h, g, w_main, w_a, w_gate, b_gate, layer, gla_layer):
    n, d = h.shape
    tm = _row_block(n, 512)
    nmain = w_main.shape[2]
    const = lambda shape, l: pl.BlockSpec((None,) + shape, lambda i: (l, 0, 0))
    row = lambda width: pl.BlockSpec((tm, width), lambda i: (i, 0))
    return pl.pallas_call(
        _gla_in_body,
        grid=(n // tm,),
        in_specs=[row(d), const((1, d), layer), const((d, nmain), gla_layer), const((d, LANES), gla_layer),
                  const((LANES, GLA_DK), gla_layer), const((1, GLA_DK), gla_layer)],
        out_specs=[row(GLA_DK), row(GLA_DK), row(GLA_DV), row(GLA_DV), row(GLA_DK)],
        out_shape=[jax.ShapeDtypeStruct((n, GLA_DK), F32), jax.ShapeDtypeStruct((n, GLA_DK), F32),
                   jax.ShapeDtypeStruct((n, GLA_DV), F32), jax.ShapeDtypeStruct((n, GLA_DV), F32),
                   jax.ShapeDtypeStruct((n, GLA_DK), F32)],
        compiler_params=_params("parallel"),
        name="gla_in",
    )(h, g, w_main, w_a, w_gate, b_gate)


def _split3(x):
    hi = x.astype(BF)
    r = x - hi.astype(F32)
    mid = r.astype(BF)
    lo = (r - mid.astype(F32)).astype(BF)
    return hi, mid, lo


def _gla_body(*refs, has_state, c, layer, stack_first):
    q_ref, k_ref, la_ref, v_ref, og_ref, gn_ref = refs[:6]
    s0_ref = refs[6] if has_state else None
    o_ref, s_out = refs[-2:]
    bb, nsub = q_ref.shape[0], q_ref.shape[1] // c
    s_ref = s_out.at[layer] if stack_first else s_out

    @pl.when(pl.program_id(1) == 0)
    def _():
        if stack_first:
            for l in range(s_out.shape[0]):
                if l != layer:
                    s_out[l] = jnp.zeros(s_out.shape[1:], s_out.dtype)
        s_ref[...] = s0_ref[...] if has_state else jnp.zeros_like(s_ref)

    r = lax.broadcasted_iota(jnp.int32, (c, c), 0)
    col = lax.broadcasted_iota(jnp.int32, (c, c), 1)
    causal = col <= r
    lower = jnp.where(causal, 1.0, 0.0).astype(BF)
    ones = jnp.ones((c, GLA_HK), BF)
    gn = gn_ref[...]

    def one(i, _):
        state = [s_ref[i, h] for h in range(GLA_HEADS)]
        for j in range(nsub):
            rows = slice(j * c, (j + 1) * c)
            parts = _split3(la_ref[i, rows, :])
            b = _dot(lower, parts[0]) + _dot(lower, parts[1]) + _dot(lower, parts[2])
            b_col = _dot_tn(parts[0], ones) + _dot_tn(parts[1], ones) + _dot_tn(parts[2], ones)
            b_last = b[c - 1:c, :]
            k = k_ref[i, rows, :]
            qg = (q_ref[i, rows, :] * jnp.exp(b)).astype(BF)
            kg = (k * jnp.exp(-b)).astype(BF)
            kd = (k * jnp.exp(b_last - b)).astype(BF)
            decay = jnp.exp(b_col)
            for h in range(GLA_HEADS):
                ks = slice(h * GLA_HK, (h + 1) * GLA_HK)
                vs = slice(h * GLA_HV, (h + 1) * GLA_HV)
                v = v_ref[i, rows, vs].astype(BF)
                a = jnp.where(causal, _dot_nt(qg[:, ks], kg[:, ks]), 0.0)
                o = _dot(a.astype(BF), v) + _dot(qg[:, ks], state[h].astype(BF))
                state[h] = (jnp.concatenate([decay[ks, :]] * (GLA_HV // GLA_HK), axis=1) * state[h]
                            + _dot_tn(kd[:, ks], v))
                og = og_ref[i, rows, vs]
                o_ref[i, rows, vs] = _rms(o, gn) * (og * jax.nn.sigmoid(og))
        for h in range(GLA_HEADS):
            s_ref[i, h] = state[h]
        return 0

    if bb == 1:
        one(0, 0)
    else:
        lax.fori_loop(0, bb // 2, lambda ii, _: one(2 * ii + 1, one(2 * ii, 0)), 0)


def _gla(q, k, la, v, og, gnorm, s0, layer, bb, c, nsub, stack=None):
    b, t, _ = q.shape
    n_layers = gnorm.shape[0]
    has_state = s0 is not None
    state = (GLA_HEADS, GLA_HK, GLA_HV)
    tok = lambda width: pl.BlockSpec((bb, c * nsub, width), lambda bi, ci: (bi, ci, 0))
    in_specs = [tok(GLA_DK), tok(GLA_DK), tok(GLA_DK), tok(GLA_DV), tok(GLA_DV),
                pl.BlockSpec((None, 1, GLA_HV), lambda bi, ci: (layer, 0, 0))]
    args = [q, k, la, v, og, gnorm]
    if has_state:
        in_specs.append(pl.BlockSpec((None, bb) + state, lambda bi, ci: (layer, bi, 0, 0, 0)))
        args.append(s0)
    aliases = {}
    if stack is None:
        sspec = pl.BlockSpec((bb,) + state, lambda bi, ci: (bi, 0, 0, 0))
        sshape = jax.ShapeDtypeStruct((b,) + state, F32)
    else:
        sshape = jax.ShapeDtypeStruct((n_layers, b) + state, F32)
        if isinstance(stack, str):
            sspec = pl.BlockSpec((n_layers, bb) + state, lambda bi, ci: (0, bi, 0, 0, 0))
        else:
            sspec = pl.BlockSpec((None, bb) + state, lambda bi, ci: (layer, bi, 0, 0, 0))
            aliases = {len(args): 1}
            in_specs.append(pl.BlockSpec(memory_space=pl.ANY))
            args.append(stack)
    return pl.pallas_call(
        functools.partial(_gla_body, has_state=has_state, c=c, layer=layer, stack_first=isinstance(stack, str)),
        grid=(b // bb, t // (c * nsub)),
        in_specs=in_specs,
        out_specs=[tok(GLA_DV), sspec],
        out_shape=[jax.ShapeDtypeStruct((b, t, GLA_DV), F32), sshape],
        input_output_aliases=aliases,
        compiler_params=_params("parallel", "arbitrary"),
        name="gla",
    )(*args)


def _ffn_body(h_ref, g_ref, wg_ref, wu_ref, wo_ref, gp_ref, wpg_ref, p_ref, wpp_ref, gf_ref, o_ref, *, final, tf):
    h = h_ref[...]
    xn = _rms(h, g_ref[...]).astype(BF)
    for f in range(wo_ref.shape[0] // tf):
        cols = slice(f * tf, (f + 1) * tf)
        gate = _dot(xn, wg_ref[:, cols])
        up = _dot(xn, wu_ref[:, cols])
        h = h + _dot((gate * jax.nn.sigmoid(gate) * up).astype(BF), wo_ref[cols, :])
    gate = jax.nn.sigmoid(_dot(_rms(h, gp_ref[...]).astype(BF), wpg_ref[...]))
    out = h + gate * _dot(p_ref[...].astype(BF), wpp_ref[...])
    o_ref[...] = _rms(out, gf_ref[...]) if final else out


def _ffn_ple(h, g, w_in, w_out, g_ple, w_gate, p, w_proj, g_final, layer, final, tf=1408):
    n, d = h.shape
    dff = w_out.shape[1]
    pd = p.shape[2]
    tm = _row_block(n, 512)
    row = pl.BlockSpec((tm, d), lambda i: (i, 0))
    once = pl.Buffered(1)
    const = lambda shape, col=0: pl.BlockSpec((None,) + shape, lambda i: (layer, 0, col), pipeline_mode=once)
    return pl.pallas_call(
        functools.partial(_ffn_body, final=final, tf=tf),
        grid=(n // tm,),
        in_specs=[row, const((1, d)), const((d, dff), 0), const((d, dff), 1), const((dff, d)),
                  const((1, d)), const((d, d)), pl.BlockSpec((None, tm, pd), lambda i: (layer, i, 0)),
                  const((pd, d)), pl.BlockSpec((1, d), lambda i: (0, 0))],
        out_specs=row,
        out_shape=jax.ShapeDtypeStruct((n, d), F32),
        compiler_params=_params("parallel"),
        name="ffn_ple",
    )(h, g, w_in, w_in, w_out, g_ple, w_gate, p, w_proj, g_final.reshape(1, d))


def _pad_rows(x, rows):
    return jnp.pad(x, ((0, 0), (0, rows - x.shape[1]), (0, 0)))


def _trunk(x, p, w, cache):
    b, t, d = x.shape
    n = b * t
    h = x.reshape(n, d)
    p = p.reshape(DEPTH, n, -1)
    new_k, new_v, new_s = [], [], []
    kv_stacks, s_stack = None, "first"
    for i in range(DEPTH):
        j = i // 2
        if i % 2 == 0:
            if cache is None:
                qa, kt, vt, kta, vtb = _qkv_prompt(h.reshape(b, t, d), w["g_mix"], w["w_sb_q_pad"], w["w_sb_kt"],
                                                   w["w_sb_vt"], w["sb_qfill"], i, j, kv_stacks)
                kv_stacks = (kt, vt)
                o = _attn_prompt(qa, kta, vtb)
            else:
                cache_kt, cache_vt, page_table, _ = cache
                q, k, v = _qkv_sample(h, w["g_mix"], w["w_sb_in"], i, j)
                new_k.append(k.reshape(b, t, d))
                new_v.append(v.reshape(b, t, d))
                pad = lambda a: _pad_rows(a.reshape(b, t, d), SUBLANES)
                o = _attn_sample(pad(q), pad(k), pad(v), cache_kt, cache_vt, j, page_table, w["b_sb"][j])[:, :t]
            h = _mm_res(o.reshape(n, d), w["w_sb_out"], j, h)
        else:
            q, k, v, og, la = _gla_in(h, w["g_mix"], w["w_gla_main"], w["w_gla_a"], w["w_gla_gate"],
                                      w["b_gla_gate"], i, j)
            shp = lambda a: a.reshape(b, t, a.shape[-1])
            if cache is None:
                c = GLA_CHUNK if t % GLA_CHUNK == 0 else t
                nsub = GLA_CHUNKS_PER_STEP if t % (c * GLA_CHUNKS_PER_STEP) == 0 else 1
                o, s = _gla(shp(q), shp(k), shp(la), shp(v), shp(og), w["g_gla_norm"], None, j, 1, c, nsub)
                new_s.append(s)
            else:
                c = 2 * SUBLANES
                pad = lambda a: _pad_rows(shp(a), c)
                o, s_stack = _gla(pad(q), pad(k), pad(la), pad(v), pad(og), w["g_gla_norm"], cache[3], j, 8, c, 1,
                                  stack=s_stack)
                o = o[:, :t]
            h = _mm_res(o.reshape(n, d), w["w_gla_out"], j, h)
        h = _ffn_ple(h, w["g_ffn"], w["w_ffn_in"], w["w_ffn_out"], w["g_ple"], w["w_ple_gate"], p, w["w_ple_proj"],
                     w["g_final"], i, i == DEPTH - 1)
    if cache is None:
        return h.reshape(b, t, d), kv_stacks[0], kv_stacks[1], jnp.stack(new_s)
    return h.reshape(b, t, d), jnp.stack(new_k), jnp.stack(new_v), s_stack


def _split_bias(bias):
    rest = bias * LOG2E
    parts = []
    for _ in range(BIAS_PARTS):
        part = rest.astype(BF).astype(F32)
        parts.append(part)
        rest = rest - part
    fill = jnp.stack(parts, axis=-1)
    fill = jnp.pad(fill, ((0, 0), (0, 0), (SB_HEAD_DIM, HEAD_PAD - SB_HEAD_DIM - BIAS_PARTS)))
    return fill.reshape(bias.shape[0], 1, SB_HEADS * HEAD_PAD)


def kernel(x_prompt, x_sample, cache_sb_k, cache_sb_v, state_gla, page_table, p_prompt, p_sample,
           g_mix, w_sb_in, w_sb_out, b_sb, w_gla_in, w_gla_gate, b_gla_gate, g_gla_norm, w_gla_out,
           g_ffn, w_ffn_in, w_ffn_out, g_ple, w_ple_gate, w_ple_proj, g_final):
    nmain = 2 * GLA_DK + 2 * GLA_DV
    n_sb = w_sb_in.shape[0]
    w_sb_bf = w_sb_in.astype(BF)
    wq_heads = w_sb_bf[:, :, :D_MODEL].reshape(n_sb, D_MODEL, SB_HEADS, SB_HEAD_DIM)
    rows = lambda a: a[:, None, :]
    w = dict(
        g_mix=rows(g_mix), b_sb=b_sb, b_gla_gate=rows(b_gla_gate), g_gla_norm=rows(g_gla_norm),
        g_ffn=rows(g_ffn), g_ple=rows(g_ple), g_final=g_final,
        w_sb_in=w_sb_bf, w_sb_out=w_sb_out.astype(BF),
        w_sb_q_pad=jnp.pad(wq_heads, ((0, 0), (0, 0), (0, 0), (0, HEAD_PAD - SB_HEAD_DIM))).reshape(
            n_sb, D_MODEL, SB_HEADS * HEAD_PAD),
        w_sb_kt=jnp.swapaxes(w_sb_bf[:, :, D_MODEL:2 * D_MODEL], 1, 2),
        w_sb_vt=jnp.swapaxes(w_sb_bf[:, :, 2 * D_MODEL:], 1, 2),
        sb_qfill=_split_bias(b_sb),
        w_gla_main=w_gla_in[:, :, :nmain].astype(BF),
        w_gla_a=jnp.pad(w_gla_in[:, :, nmain:], ((0, 0), (0, 0), (0, LANES - GLA_GATE_RANK))).astype(BF),
        w_gla_gate=jnp.pad(w_gla_gate, ((0, 0), (0, LANES - GLA_GATE_RANK), (0, 0))).astype(BF),
        w_gla_out=w_gla_out.astype(BF), w_ffn_in=w_ffn_in.astype(BF), w_ffn_out=w_ffn_out.astype(BF),
        w_ple_gate=w_ple_gate.astype(BF), w_ple_proj=w_ple_proj.astype(BF),
    )
    n_layers, n_pool = cache_sb_k.shape[0], cache_sb_k.shape[1]
    kt_view = lambda c: jnp.transpose(c, (0, 1, 3, 4, 2)).reshape(n_layers, n_pool, D_MODEL, PAGE_SIZE)
    cache = (kt_view(cache_sb_k), kt_view(cache_sb_v), page_table, state_gla)
    y_prompt, kt_prompt, vt_prompt, s_prompt = _trunk(x_prompt, p_prompt, w, None)
    y_sample, k_sample, v_sample, s_sample = _trunk(x_sample, p_sample, w, cache)
    bp, tp = x_prompt.shape[0], x_prompt.shape[1]
    bs, ts = x_sample.shape[0], x_sample.shape[1]
    heads_t = lambda a: jnp.transpose(a.reshape(n_sb, bp, SB_HEADS, SB_HEAD_DIM, tp), (0, 1, 4, 2, 3))
    heads = lambda a: a.reshape(n_sb, bs, ts, SB_HEADS, SB_HEAD_DIM)
    return (y_prompt, y_sample, heads_t(kt_prompt), heads_t(vt_prompt), s_prompt,
            heads(k_sample), heads(v_sample), s_sample)
```
